```python
import jax, jax.numpy as jnp
from jax import lax
import numpy as np

D_MODEL = 1024
BATCH = 16
SEQ = 2048
DEPTH = 2
DEC_BATCH = 4
DEC_SEQ = 4096
PAST_LEN = 128

HEAD_DIM = 64
MIX_WIDTH = D_MODEL
ATTN_HEADS = (MIX_WIDTH // 2) // HEAD_DIM
ATTN_WIDTH = ATTN_HEADS * HEAD_DIM
GLA_HEADS = (MIX_WIDTH - ATTN_WIDTH) // HEAD_DIM
GLA_DV = HEAD_DIM
GLA_DK = HEAD_DIM // 2
GLA_KW = GLA_HEADS * GLA_DK
GLA_VW = GLA_HEADS * GLA_DV
GATE_RANK = 16
GATE_TAU = 16.0
GLA_CHUNK = 32
DIL_PAIRS = ((128, 1), (512, 4), (2048, 16))
D_FF = ((8 * D_MODEL // 3 + 127) // 128) * 128
CONV_WIDTH = 3
EPS = 1e-6
NEG = -1e30
IN_SIZES = (ATTN_WIDTH, ATTN_WIDTH, ATTN_WIDTH, GLA_KW, GLA_KW, GLA_VW, GLA_VW, GATE_RANK)
IN_WIDTH = sum(IN_SIZES)
IN_OFFSETS = [int(v) for v in np.cumsum(IN_SIZES)[:-1]]

kernel_name = "hybrid_dilated_gla_encoder"


def rmsnorm(x, g):
    xf = x.astype(jnp.float32)
    y = xf * lax.rsqrt(jnp.mean(xf * xf, axis=-1, keepdims=True) + EPS)
    return (y * g.astype(jnp.float32)).astype(x.dtype)


def alibi_slopes(n):
    return jnp.exp2(-8.0 * jnp.arange(1, n + 1, dtype=jnp.float32) / n)


def dilated_branch(q, k, v, window, dilation, slopes):
    B, S, H, Dh = q.shape
    n_side = window // (2 * dilation)
    blk = n_side
    L = S // dilation
    nb = -(-L // blk)
    Lp = nb * blk

    def to_blocks(t):
        t = t.reshape(B, L, dilation, H, Dh)
        t = jnp.pad(t, ((0, 0), (0, Lp - L), (0, 0), (0, 0), (0, 0)))
        return t.reshape(B, nb, blk, dilation, H, Dh)

    def neighbours(t):
        tp = jnp.pad(t, ((0, 0), (1, 1), (0, 0), (0, 0), (0, 0), (0, 0)))
        return jnp.concatenate([tp[:, :-2], tp[:, 1:-1], tp[:, 2:]], axis=2)

    qb = to_blocks(q)
    kn = neighbours(to_blocks(k))
    vn = neighbours(to_blocks(v))
    s = jnp.einsum('bnqrhd,bnkrhd->bnrhqk', qb, kn).astype(jnp.float32) * (Dh ** -0.5)
    qi = jnp.arange(blk)
    kj = jnp.arange(3 * blk)
    delta = kj[None, :] - blk - qi[:, None]
    key_m = jnp.arange(nb)[:, None, None] * blk + delta[None]
    valid = (jnp.abs(delta) <= n_side)[None] & (key_m >= 0) & (key_m < L)
    dist = (jnp.abs(delta) * dilation).astype(jnp.float32)
    bias = -slopes[:, None, None] * dist[None]
    s = jnp.where(valid[None, :, None, None], s + bias[None, None, None], NEG)
    m = jnp.max(s, axis=-1, keepdims=True)
    p = jnp.exp(s - m)
    den = jnp.sum(p, axis=-1, keepdims=True)
    lse = (m + jnp.log(den))[..., 0]
    o = jnp.einsum('bnrhqk,bnkrhd->bnqrhd', (p / den).astype(v.dtype), vn)
    o = o.reshape(B, Lp, dilation, H, Dh)[:, :L].reshape(B, S, H, Dh)
    lse = lse.transpose(0, 1, 4, 2, 3).reshape(B, Lp, dilation, H)[:, :L].reshape(B, S, H)
    return o, lse


def dilated_attention(q, k, v):
    slopes = alibi_slopes(q.shape[2])
    outs, lses = [], []
    for window, dilation in DIL_PAIRS:
        o, l = dilated_branch(q, k, v, window, dilation, slopes)
        outs.append(o.astype(jnp.float32))
        lses.append(l)
    w = jax.nn.softmax(jnp.stack(lses, axis=0), axis=0)
    return jnp.sum(w[..., None] * jnp.stack(outs, axis=0), axis=0)


def gla_direction(q, k, v, log_a):
    B, S, H, dk = q.shape
    dv = v.shape[-1]
    C = GLA_CHUNK
    N = S // C
    qc = q.reshape(B, N, C, H, dk)
    kc = k.reshape(B, N, C, H, dk)
    vc = v.reshape(B, N, C, H, dv)
    b = jnp.cumsum(log_a.reshape(B, N, C, H, dk), axis=2)
    b_last = b[:, :, -1]
    q_in = qc * jnp.exp(b)
    k_intra = kc * jnp.exp(-b)
    k_state = kc * jnp.exp(b_last[:, :, None] - b)
    causal = jnp.tril(jnp.ones((C, C), dtype=bool))
    att = jnp.where(causal, jnp.einsum('bnihk,bnjhk->bnhij', q_in, k_intra), 0.0)
    o_intra = jnp.einsum('bnhij,bnjhv->bnihv', att, vc)
    u = jnp.einsum('bnjhk,bnjhv->bnhkv', k_state, vc)

    def step(state, inp):
        decay, u_n = inp
        return decay[..., None] * state + u_n, state

    _, s_prev = lax.scan(step, jnp.zeros((B, H, dk, dv), jnp.float32),
                         (jnp.exp(b_last).transpose(1, 0, 2, 3), u.transpose(1, 0, 2, 3, 4)))
    s_prev = s_prev.transpose(1, 0, 2, 3, 4)
    o_inter = jnp.einsum('bnihk,bnhkv->bnihv', q_in, s_prev)
    return (o_intra + o_inter).reshape(B, S, H, dv)


def gla_mixer(q, k, v, r, lr, wg_f, bg_f, wg_b, bg_b, g_norm):
    B, S, _ = q.shape
    f32 = jnp.float32
    qh = q.astype(f32).reshape(B, S, GLA_HEADS, GLA_DK) * (GLA_DK ** -0.5)
    kh = k.astype(f32).reshape(B, S, GLA_HEADS, GLA_DK)
    vh = v.astype(f32).reshape(B, S, GLA_HEADS, GLA_DV)
    la_f = (jax.nn.log_sigmoid((lr @ wg_f + bg_f).astype(f32)) / GATE_TAU).reshape(B, S, GLA_HEADS, GLA_DK)
    la_b = (jax.nn.log_sigmoid((lr @ wg_b + bg_b).astype(f32)) / GATE_TAU).reshape(B, S, GLA_HEADS, GLA_DK)
    o_f = gla_direction(qh, kh, vh, la_f)
    o_b = jnp.flip(gla_direction(jnp.flip(qh, 1), jnp.flip(kh, 1), jnp.flip(vh, 1), jnp.flip(la_b, 1)), 1)
    o = rmsnorm(o_f + o_b, g_norm)
    o = o * jax.nn.silu(r.astype(f32).reshape(B, S, GLA_HEADS, GLA_DV))
    return o.reshape(B, S, GLA_VW)


def dwconv_centred(a, w, b):
    S = a.shape[1]
    pad = CONV_WIDTH // 2
    ap = jnp.pad(a, ((0, 0), (pad, pad), (0, 0)))
    y = b
    for j in range(CONV_WIDTH):
        y = y + ap[:, j:j + S] * w[j]
    return y


def encoder_layer(x, n_pre, w_in, wg_f, bg_f, wg_b, bg_b, g_gla, w_out, n_post,
                  n_ffn_pre, w_up, conv_w, conv_b, w_down, n_ffn_post):
    B, S, _ = x.shape
    h = rmsnorm(x, n_pre)
    aq, ak, av, gq, gk, gv, gr, glr = jnp.split(h @ w_in, IN_OFFSETS, axis=-1)
    heads = lambda t: t.reshape(B, S, ATTN_HEADS, HEAD_DIM)
    o_attn = dilated_attention(heads(aq), heads(ak), heads(av)).reshape(B, S, ATTN_WIDTH)
    o_gla = gla_mixer(gq, gk, gv, gr, glr, wg_f, bg_f, wg_b, bg_b, g_gla)
    mix = jnp.concatenate([o_attn, o_gla], axis=-1).astype(x.dtype) @ w_out
    x = x + rmsnorm(mix, n_post)
    h = rmsnorm(x, n_ffn_pre)
    a, g = jnp.split(h @ w_up, 2, axis=-1)
    a = dwconv_centred(a, conv_w, conv_b)
    f = (jax.nn.gelu(a, approximate=True) * g) @ w_down
    return x + rmsnorm(f, n_ffn_post)


def setup_inputs(seed: int = 0) -> dict:
    key = jax.random.key(seed)
    ks = jax.random.split(key, 20)
    nrm = lambda k, shape, s: jax.random.normal(k, shape, jnp.float32) * s
    gain = lambda k, shape: 1.0 + 0.05 * jax.random.normal(k, shape, jnp.float32)
    return {
        "x_prompt": nrm(ks[0], (BATCH, SEQ, D_MODEL), 1.0),
        "x_sample": nrm(ks[1], (DEC_BATCH, DEC_SEQ, D_MODEL), 1.0),
        "norm_mix_pre": gain(ks[2], (DEPTH, D_MODEL)),
        "w_in": nrm(ks[3], (DEPTH, D_MODEL, IN_WIDTH), D_MODEL ** -0.5),
        "w_gate_fwd": nrm(ks[4], (DEPTH, GATE_RANK, GLA_KW), GATE_RANK ** -0.5),
        "b_gate_fwd": nrm(ks[5], (DEPTH, GLA_KW), 0.1),
        "w_gate_bwd": nrm(ks[6], (DEPTH, GATE_RANK, GLA_KW), GATE_RANK ** -0.5),
        "b_gate_bwd": nrm(ks[7], (DEPTH, GLA_KW), 0.1),
        "gla_norm": gain(ks[8], (DEPTH, GLA_DV)),
        "w_out": nrm(ks[9], (DEPTH, MIX_WIDTH, D_MODEL), MIX_WIDTH ** -0.5),
        "norm_mix_post": gain(ks[10], (DEPTH, D_MODEL)),
        "norm_ffn_pre": gain(ks[11], (DEPTH, D_MODEL)),
        "w_up": nrm(ks[12], (DEPTH, D_MODEL, 2 * D_FF), D_MODEL ** -0.5),
        "conv_w": nrm(ks[13], (DEPTH, CONV_WIDTH, D_FF), CONV_WIDTH ** -0.5),
        "conv_b": nrm(ks[14], (DEPTH, D_FF), 0.02),
        "w_down": nrm(ks[15], (DEPTH, D_FF, D_MODEL), D_FF ** -0.5),
        "norm_ffn_post": gain(ks[16], (DEPTH, D_MODEL)),
    }


def reference(x_prompt, x_sample, norm_mix_pre, w_in, w_gate_fwd, b_gate_fwd, w_gate_bwd, b_gate_bwd,
              gla_norm, w_out, norm_mix_post, norm_ffn_pre, w_up, conv_w, conv_b, w_down, norm_ffn_post):
    def trunk(x):
        for l in range(DEPTH):
            x = encoder_layer(x, norm_mix_pre[l], w_in[l], w_gate_fwd[l], b_gate_fwd[l], w_gate_bwd[l],
                              b_gate_bwd[l], gla_norm[l], w_out[l], norm_mix_post[l], norm_ffn_pre[l],
                              w_up[l], conv_w[l], conv_b[l], w_down[l], norm_ffn_post[l])
        return x

    y_prompt = trunk(x_prompt)
    y_sample = trunk(x_sample)
    return (y_prompt, y_sample)
```

```python
import functools

import numpy as np
import jax
import jax.numpy as jnp
from jax import lax
from jax.experimental import pallas as pl
from jax.experimental.pallas import tpu as pltpu

F32 = jnp.float32
BF16 = jnp.bfloat16

D_MODEL = 1024
HEAD_DIM = 64
ATTN_HEADS = 8
ATTN_WIDTH = ATTN_HEADS * HEAD_DIM
GLA_HEADS = 8
GLA_DK = 32
GLA_DV = 64
GLA_KW = GLA_HEADS * GLA_DK
GLA_VW = GLA_HEADS * GLA_DV
GATE_RANK = 16
GATE_TAU = 16.0
GLA_CHUNK = 32
DIL_PAIRS = ((128, 1), (512, 4), (2048, 16))
N_SIDE = 64
D_FF = 2816
EPS = 1e-6
MASKED_DIST = 1e30

LANES = 128
HEAD_PAIRS = ATTN_HEADS // 2
VMEM_LIMIT = 56 * 1024 * 1024

TM_PROJ = 512
ATT_BQ = 128
GLA_T = 256
FF_CHUNK = 256


def _cparams(*sem):
    return pltpu.CompilerParams(dimension_semantics=sem, vmem_limit_bytes=VMEM_LIMIT)


def _rms(x, g):
    ms = jnp.mean(x * x, axis=-1, keepdims=True)
    return x * lax.rsqrt(ms + EPS) * g


def _const_spec(shape):
    nd = len(shape)
    return pl.BlockSpec(shape, lambda *_: (0,) * nd)


def _inproj_kernel(x_ref, g_ref, wm_ref, wlr_ref, wg_ref, bg_ref,
                   q_ref, k_ref, v_ref, gq_ref, gk_ref, gv_ref, gr_ref, laf_ref, lab_ref):
    h = _rms(x_ref[...], g_ref[...]).astype(BF16)

    def proj(lo, hi):
        return jnp.dot(h, wm_ref[:, lo:hi], preferred_element_type=F32)

    a = ATTN_WIDTH
    q_ref[...] = (proj(0, a) * (HEAD_DIM ** -0.5)).astype(BF16)
    k_ref[...] = proj(a, 2 * a).astype(BF16)
    v_ref[...] = proj(2 * a, 3 * a).astype(BF16)
    o = 3 * a
    gq_ref[...] = proj(o, o + GLA_KW)
    gk_ref[...] = proj(o + GLA_KW, o + 2 * GLA_KW)
    o += 2 * GLA_KW
    gv_ref[...] = proj(o, o + GLA_VW)
    gr_ref[...] = proj(o + GLA_VW, o + 2 * GLA_VW)
    lr = jnp.dot(h, wlr_ref[...], preferred_element_type=F32)
    pre = jnp.dot(lr, wg_ref[...], preferred_element_type=F32,
                  precision=lax.Precision.HIGHEST) + bg_ref[...]
    la = (jnp.minimum(pre, 0.0) - jnp.log(1.0 + jnp.exp(-jnp.abs(pre)))) * (1.0 / GATE_TAU)
    laf_ref[...] = la[:, :GLA_KW]
    lab_ref[...] = la[:, GLA_KW:]


def _inproj(x2, g, wm, wlr, wg, bg):
    n = x2.shape[0]
    tm = TM_PROJ
    row = lambda w: pl.BlockSpec((tm, w), lambda i: (i, 0))
    outs = [(ATTN_WIDTH, BF16)] * 3 + [(GLA_KW, F32)] * 2 + [(GLA_VW, F32)] * 2 + [(GLA_KW, F32)] * 2
    return pl.pallas_call(
        _inproj_kernel,
        grid=(n // tm,),
        in_specs=[row(D_MODEL), _const_spec(g.shape), _const_spec(wm.shape), _const_spec(wlr.shape),
                  _const_spec(wg.shape), _const_spec(bg.shape)],
        out_specs=[row(w) for w, _ in outs],
        out_shape=[jax.ShapeDtypeStruct((n, w), dt) for w, dt in outs],
        compiler_params=_cparams("parallel"),
        name="inproj",
    )(x2, g, wm, wlr, wg, bg)


def _attn_kernel(slope_ref, dist_ref, pad_ref, q_ref, k_ref, v_ref, o_ref, lse_ref, *, seq, bk, dil):
    lane = lax.broadcasted_iota(jnp.int32, (1, LANES), 1)
    first_head = lane < HEAD_DIM
    c = slope_ref[0] * float(dil)
    nqb = seq // ATT_BQ

    def block(m0, ks, dist, pad_dist):
        q = q_ref[0, pl.ds(m0, ATT_BQ), :]
        k = k_ref[0, pl.ds(ks, bk), :]
        v = v_ref[0, pl.ds(ks, bk), :]
        res = []
        for hh in range(2):
            hm = first_head if hh == 0 else jnp.logical_not(first_head)
            ch = c[:, hh * HEAD_DIM:hh * HEAD_DIM + 1]
            qm = jnp.where(hm, q, jnp.zeros_like(q))
            s = lax.dot_general(qm, k, (((1,), (1,)), ((), ())), preferred_element_type=F32)
            s = s - ch * dist
            m = jnp.max(s, axis=-1, keepdims=True)
            if pad_dist is not None:
                s_pad = -ch * pad_dist
                m = jnp.maximum(m, jnp.max(s_pad, axis=-1, keepdims=True))
            p = jnp.exp(s - m)
            den = jnp.sum(p, axis=-1, keepdims=True)
            if pad_dist is not None:
                den = den + jnp.sum(jnp.exp(s_pad - m), axis=-1, keepdims=True)
            pv = jnp.dot(p.astype(BF16), v, preferred_element_type=F32)
            res.append((pv / den, m + jnp.log(den)))
        o_ref[0, pl.ds(m0, ATT_BQ), :] = jnp.where(first_head, res[0][0], res[1][0])
        lse_ref[0, pl.ds(m0, ATT_BQ), :] = jnp.where(first_head, res[0][1], res[1][1])

    def inner(qb, carry):
        m0 = pl.multiple_of(qb * ATT_BQ, ATT_BQ)
        ks = pl.multiple_of(jnp.maximum(m0 - N_SIDE, 0), N_SIDE)
        block(m0, ks, dist_ref[jnp.minimum(qb, 1)], None)
        return carry

    lax.fori_loop(0, nqb - 1, inner, 0)
    block(seq - ATT_BQ, seq - bk, dist_ref[dist_ref.shape[0] - 1], pad_ref[...])


def _dist_tables(seq, bk):
    i = np.arange(ATT_BQ)[:, None]
    j = np.arange(bk)[None, :]

    def table(off, first):
        delta = j - off - i
        ok = np.abs(delta) <= N_SIDE
        if first:
            ok &= (delta >= 0) | (i >= N_SIDE)
        return np.where(ok, np.abs(delta), MASKED_DIST)

    if seq == ATT_BQ:
        kinds = [table(0, True)]
    else:
        kinds = [table(0, True), table(N_SIDE, False), table(bk - ATT_BQ, False)]
    t = np.arange(LANES)[None, :]
    pad = np.where((t < N_SIDE) & (i >= N_SIDE) & (t >= ATT_BQ - i), t, MASKED_DIST)
    return jnp.asarray(np.stack(kinds).astype(np.float32)), jnp.asarray(pad.astype(np.float32))


def _attn_branch(q, k, v, slopes, dil):
    b, s, _ = q.shape
    seq = s // dil
    assert seq % ATT_BQ == 0
    bk = min(ATT_BQ + 2 * N_SIDE, seq)
    view = lambda t: t.reshape(b, seq, dil * ATTN_WIDTH)
    blk = pl.BlockSpec((1, seq, LANES), lambda bi, r, p: (bi, 0, r * HEAD_PAIRS + p))
    dist, pad = _dist_tables(seq, bk)
    o, lse = pl.pallas_call(
        functools.partial(_attn_kernel, seq=seq, bk=bk, dil=dil),
        grid=(b, dil, HEAD_PAIRS),
        in_specs=[pl.BlockSpec((1, 1, LANES), lambda bi, r, p: (p, 0, 0)), _const_spec(dist.shape),
                  _const_spec(pad.shape), blk, blk, blk],
        out_specs=[blk, blk],
        out_shape=[jax.ShapeDtypeStruct((b, seq, dil * ATTN_WIDTH), F32)] * 2,
        compiler_params=_cparams("parallel", "parallel", "parallel"),
        name=f"attn_d{dil}",
    )(slopes, dist, pad, view(q), view(k), view(v))
    return o.reshape(b * s, ATTN_WIDTH), lse.reshape(b * s, ATTN_WIDTH)


def _split3(x):
    hi = x.astype(BF16)
    r = x - hi.astype(F32)
    mid = r.astype(BF16)
    lo = (r - mid.astype(F32)).astype(BF16)
    return hi, mid, lo


def _gla_kernel(*refs, reverse):
    if reverse:
        (tri_ref, triT_ref, q_ref, k_ref, v_ref, la_ref, of_ref, r_ref, gn_ref, o_ref, state_ref) = refs
    else:
        (tri_ref, triT_ref, q_ref, k_ref, v_ref, la_ref, o_ref, state_ref) = refs
    T, C, KW = GLA_T, GLA_CHUNK, GLA_KW
    nch = T // C

    @pl.when(pl.program_id(1) == 0)
    def _():
        state_ref[...] = jnp.zeros_like(state_ref)

    la = la_ref[0]
    b = sum(jnp.dot(tri_ref[...], t, preferred_element_type=F32) for t in _split3(la))
    laT = la.T
    bt = sum(jnp.dot(t, triT_ref[...], preferred_element_type=F32) for t in _split3(laT))
    bT, totT = bt[:, :T], bt[:, T:]
    kT = k_ref[0].T
    q_in = q_ref[0] * jnp.exp(b) * (GLA_DK ** -0.5)
    k_intraT = (kT * jnp.exp(-bT)).astype(BF16)
    k_stateT = kT * jnp.exp(totT - bT)
    decayT = jnp.exp(totT)
    v = v_ref[0].astype(BF16)

    lane_k = lax.broadcasted_iota(jnp.int32, (1, KW), 1)
    lane_v = lax.broadcasted_iota(jnp.int32, (1, LANES), 1)
    lane_t = lax.broadcasted_iota(jnp.int32, (1, T), 1)
    ti = lax.broadcasted_iota(jnp.int32, (T, T), 0)
    tj = lax.broadcasted_iota(jnp.int32, (T, T), 1)
    causal = (ti // C == tj // C) & ((tj >= ti) if reverse else (tj <= ti))

    intra = []
    for p in range(GLA_HEADS // 2):
        acc = None
        for hh in range(2):
            h = 2 * p + hh
            qm = jnp.where(lane_k // GLA_DK == h, q_in, 0.0).astype(BF16)
            s = jnp.dot(qm, k_intraT, preferred_element_type=F32)
            att = jnp.where(causal, s, 0.0).astype(BF16)
            pv = jnp.dot(att, v[:, p * LANES:(p + 1) * LANES], preferred_element_type=F32)
            pv = jnp.where(lane_v // GLA_DV == hh, pv, 0.0)
            acc = pv if acc is None else acc + pv
        intra.append(acc)

    par_row = lax.broadcasted_iota(jnp.int32, (2 * GLA_DK, LANES), 0) // GLA_DK
    par_lane = lax.broadcasted_iota(jnp.int32, (2 * GLA_DK, LANES), 1) // GLA_DV
    parity = par_row == par_lane
    u = [[None] * (GLA_HEADS // 2) for _ in range(nch)]
    for p in range(GLA_HEADS // 2):
        rows = k_stateT[p * 2 * GLA_DK:(p + 1) * 2 * GLA_DK, :]
        lhs = jnp.concatenate([jnp.where(lane_t // C == n, rows, 0.0) for n in range(nch)], axis=0).astype(BF16)
        up = jnp.dot(lhs, v[:, p * LANES:(p + 1) * LANES], preferred_element_type=F32)
        for n in range(nch):
            u[n][p] = jnp.where(parity, up[n * 2 * GLA_DK:(n + 1) * 2 * GLA_DK, :], 0.0)

    state = state_ref[...]
    seen = [None] * nch
    order = range(nch - 1, -1, -1) if reverse else range(nch)
    for n in order:
        seen[n] = state.astype(BF16)
        dcol = jnp.broadcast_to(decayT[:, n * C:n * C + 1], (KW, LANES))
        state = dcol * state + jnp.concatenate(u[n], axis=0)
    state_ref[...] = state

    inter = [[None] * nch for _ in range(GLA_HEADS // 2)]
    for n in range(nch):
        qn = q_in[n * C:(n + 1) * C, :]
        lhs = jnp.concatenate([jnp.where(lane_k // (2 * GLA_DK) == p, qn, 0.0) for p in range(GLA_HEADS // 2)],
                              axis=0).astype(BF16)
        res = jnp.dot(lhs, seen[n], preferred_element_type=F32)
        for p in range(GLA_HEADS // 2):
            inter[p][n] = res[p * C:(p + 1) * C, :]
    o = jnp.concatenate([intra[p] + jnp.concatenate(inter[p], axis=0) for p in range(GLA_HEADS // 2)], axis=1)

    if not reverse:
        o_ref[0] = o
        return
    o = o + of_ref[0]
    r = r_ref[0]
    outs = []
    for p in range(GLA_HEADS // 2):
        x = o[:, p * LANES:(p + 1) * LANES]
        sq = x * x
        first = lane_v < GLA_DV
        s0 = jnp.sum(jnp.where(first, sq, 0.0), axis=-1, keepdims=True)
        s1 = jnp.sum(jnp.where(first, 0.0, sq), axis=-1, keepdims=True)
        ms = jnp.where(first, s0, s1) * (1.0 / GLA_DV)
        y = x * lax.rsqrt(ms + EPS) * gn_ref[...]
        rp = r[:, p * LANES:(p + 1) * LANES]
        outs.append(y * rp / (1.0 + jnp.exp(-rp)))
    o_ref[0] = jnp.concatenate(outs, axis=1).astype(o_ref.dtype)


def _gla_consts(reverse):
    i = np.arange(GLA_T)
    same = (i[:, None] // GLA_CHUNK) == (i[None, :] // GLA_CHUNK)
    tri = same & ((i[None, :] >= i[:, None]) if reverse else (i[None, :] <= i[:, None]))
    triT = np.concatenate([tri.T, same], axis=1)
    return jnp.asarray(tri, BF16), jnp.asarray(triT, BF16)


def _gla_dir(gq, gk, gv, la, reverse, o_fwd=None, gr=None, gnorm=None):
    b, s, _ = gq.shape
    nt = s // GLA_T
    tile = (lambda bi, t: (bi, nt - 1 - t, 0)) if reverse else (lambda bi, t: (bi, t, 0))
    kw = pl.BlockSpec((1, GLA_T, GLA_KW), tile)
    vw = pl.BlockSpec((1, GLA_T, GLA_VW), tile)
    tri, triT = _gla_consts(reverse)
    args = [tri, triT, gq, gk, gv, la]
    specs = [_const_spec(tri.shape), _const_spec(triT.shape), kw, kw, vw, kw]
    if reverse:
        args += [o_fwd, gr, gnorm]
        specs += [vw, vw, _const_spec(gnorm.shape)]
    return pl.pallas_call(
        functools.partial(_gla_kernel, reverse=reverse),
        grid=(b, nt),
        in_specs=specs,
        out_specs=vw,
        out_shape=jax.ShapeDtypeStruct((b, s, GLA_VW), BF16 if reverse else F32),
        scratch_shapes=[pltpu.VMEM((GLA_KW, LANES), F32)],
        compiler_params=_cparams("parallel", "arbitrary"),
        name="gla_bwd" if reverse else "gla_fwd",
    )(*args)


def _outproj_kernel(x_ref, o1_ref, o2_ref, o3_ref, l1_ref, l2_ref, l3_ref, og_ref, wa_ref, wg_ref, g_ref, y_ref):
    l1, l2, l3 = l1_ref[...], l2_ref[...], l3_ref[...]
    m = jnp.maximum(jnp.maximum(l1, l2), l3)
    e1, e2, e3 = jnp.exp(l1 - m), jnp.exp(l2 - m), jnp.exp(l3 - m)
    oa = (e1 * o1_ref[...] + e2 * o2_ref[...] + e3 * o3_ref[...]) / (e1 + e2 + e3)
    mix = jnp.dot(oa.astype(BF16), wa_ref[...], preferred_element_type=F32)
    mix = mix + jnp.dot(og_ref[...], wg_ref[...], preferred_element_type=F32)
    y_ref[...] = x_ref[...] + _rms(mix, g_ref[...])


def _outproj(x2, outs, lses, og, wa, wg, g):
    n = x2.shape[0]
    tm = TM_PROJ
    row = lambda w: pl.BlockSpec((tm, w), lambda i: (i, 0))
    return pl.pallas_call(
        _outproj_kernel,
        grid=(n // tm,),
        in_specs=[row(D_MODEL)] + [row(ATTN_WIDTH)] * 6 + [row(GLA_VW), _const_spec(wa.shape), _const_spec(wg.shape),
                                                          _const_spec(g.shape)],
        out_specs=row(D_MODEL),
        out_shape=jax.ShapeDtypeStruct((n, D_MODEL), F32),
        compiler_params=_cparams("parallel"),
        name="outproj",
    )(x2, *outs, *lses, og, wa, wg, g)


def _gelu_tanh(y):
    return 0.5 * y * (1.0 + jnp.tanh(0.7978845608028654 * (y + 0.044715 * (y * y * y))))


def _ffn_kernel(x_ref, xp_ref, xn_ref, g_ref, wup_ref, cw_ref, cb_ref, wdn_ref, gp_ref, y_ref, acc_ref, *, tiles_per_seq):
    tm = x_ref.shape[0]
    i = pl.program_id(0)
    first = (i % tiles_per_seq) == 0
    last = (i % tiles_per_seq) == tiles_per_seq - 1
    x = x_ref[...]
    g = g_ref[...]
    h = _rms(x, g).astype(BF16)
    halo = _rms(jnp.concatenate([xp_ref[...], xn_ref[...]], axis=0), g).astype(BF16)
    row = lax.broadcasted_iota(jnp.int32, (tm, 1), 0)
    for c in range(D_FF // FF_CHUNK):
        lo, hi = c * FF_CHUNK, (c + 1) * FF_CHUNK
        wa = wup_ref[:, lo:hi]
        a = jnp.dot(h, wa, preferred_element_type=F32)
        gate = jnp.dot(h, wup_ref[:, D_FF + lo:D_FF + hi], preferred_element_type=F32)
        ah = jnp.dot(halo, wa, preferred_element_type=F32)
        a_prev = jnp.where(first, 0.0, ah[7:8, :])
        a_next = jnp.where(last, 0.0, ah[8:9, :])
        a_dn = jnp.where(row == 0, a_prev, pltpu.roll(a, 1, axis=0))
        a_up = jnp.where(row == tm - 1, a_next, pltpu.roll(a, tm - 1, axis=0))
        cw = cw_ref[:, lo:hi]
        y = cb_ref[:, lo:hi] + a_dn * cw[0:1, :] + a * cw[1:2, :] + a_up * cw[2:3, :]
        act = (_gelu_tanh(y) * gate).astype(BF16)
        part = jnp.dot(act, wdn_ref[lo:hi, :], preferred_element_type=F32)
        if c == 0:
            acc_ref[...] = part
        else:
            acc_ref[...] += part
    y_ref[...] = x + _rms(acc_ref[...], gp_ref[...])


def _ffn(x2, seq, g, wup, cw, cb, wdn, gp):
    n = x2.shape[0]
    tm = TM_PROJ
    assert seq % tm == 0
    hb = tm // 8
    nblk8 = n // 8
    return pl.pallas_call(
        functools.partial(_ffn_kernel, tiles_per_seq=seq // tm),
        grid=(n // tm,),
        in_specs=[pl.BlockSpec((tm, D_MODEL), lambda i: (i, 0)),
                  pl.BlockSpec((8, D_MODEL), lambda i: (jnp.maximum(i * hb - 1, 0), 0)),
                  pl.BlockSpec((8, D_MODEL), lambda i: (jnp.minimum((i + 1) * hb, nblk8 - 1), 0)),
                  _const_spec(g.shape), _const_spec(wup.shape), _const_spec(cw.shape), _const_spec(cb.shape),
                  _const_spec(wdn.shape), _const_spec(gp.shape)],
        out_specs=pl.BlockSpec((tm, D_MODEL), lambda i: (i, 0)),
        out_shape=jax.ShapeDtypeStruct((n, D_MODEL), F32),
        scratch_shapes=[pltpu.VMEM((tm, D_MODEL), F32)],
        compiler_params=_cparams("parallel"),
        name="ffn",
    )(x2, x2, x2, g, wup, cw, cb, wdn, gp)


def _layer_params(l, norm_mix_pre, w_in, w_gate_fwd, b_gate_fwd, w_gate_bwd, b_gate_bwd, gla_norm, w_out,
                  norm_mix_post, norm_ffn_pre, w_up, conv_w, conv_b, w_down, norm_ffn_post):
    n_main = 3 * ATTN_WIDTH + 2 * GLA_KW + 2 * GLA_VW
    wl = w_in[l]
    row = lambda t: t.reshape(1, -1).astype(F32)
    wlr = jnp.pad(wl[:, n_main:], ((0, 0), (0, LANES - GATE_RANK))).astype(BF16)
    wg = jnp.pad(jnp.concatenate([w_gate_fwd[l], w_gate_bwd[l]], axis=1), ((0, LANES - GATE_RANK), (0, 0)))
    bg = jnp.concatenate([b_gate_fwd[l], b_gate_bwd[l]]).reshape(1, -1)
    slopes = jnp.exp2(-8.0 * jnp.arange(1, ATTN_HEADS + 1, dtype=F32) / ATTN_HEADS)
    return dict(
        n_pre=row(norm_mix_pre[l]), wm=wl[:, :n_main].astype(BF16), wlr=wlr, wg=wg.astype(F32), bg=bg.astype(F32),
        slopes=jnp.repeat(slopes, HEAD_DIM).reshape(HEAD_PAIRS, 1, LANES),
        gnorm=jnp.tile(gla_norm[l].astype(F32), 2).reshape(1, LANES),
        wo_a=w_out[l][:ATTN_WIDTH].astype(BF16), wo_g=w_out[l][ATTN_WIDTH:].astype(BF16), n_post=row(norm_mix_post[l]),
        n_ffn_pre=row(norm_ffn_pre[l]), wup=w_up[l].astype(BF16), cw=conv_w[l].astype(F32), cb=row(conv_b[l]),
        wdn=w_down[l].astype(BF16), n_ffn_post=row(norm_ffn_post[l]),
    )


def _layer(x, p):
    b, s, _ = x.shape
    x2 = x.reshape(b * s, D_MODEL)
    q, k, v, gq, gk, gv, gr, laf, lab = _inproj(x2, p["n_pre"], p["wm"], p["wlr"], p["wg"], p["bg"])
    seq3 = lambda t: t.reshape(b, s, t.shape[-1])
    outs, lses = [], []
    for _, dil in DIL_PAIRS:
        o, lse = _attn_branch(seq3(q), seq3(k), seq3(v), p["slopes"], dil)
        outs.append(o)
        lses.append(lse)
    o_fwd = _gla_dir(seq3(gq), seq3(gk), seq3(gv), seq3(laf), reverse=False)
    o_gla = _gla_dir(seq3(gq), seq3(gk), seq3(gv), seq3(lab), reverse=True, o_fwd=o_fwd, gr=seq3(gr), gnorm=p["gnorm"])
    x2 = _outproj(x2, outs, lses, o_gla.reshape(b * s, GLA_VW), p["wo_a"], p["wo_g"], p["n_post"])
    x2 = _ffn(x2, s, p["n_ffn_pre"], p["wup"], p["cw"], p["cb"], p["wdn"], p["n_ffn_post"])
    return x2.reshape(b, s, D_MODEL)


def kernel(x_prompt, x_sample, norm_mix_pre, w_in, w_gate_fwd, b_gate_fwd, w_gate_bwd, b_gate_bwd, gla_norm, w_out,
           norm_mix_post, norm_ffn_pre, w_up, conv_w, conv_b, w_down, norm_ffn_post):
    weights = (norm_mix_pre, w_in, w_gate_fwd, b_gate_fwd, w_gate_bwd, b_gate_bwd, gla_norm, w_out, norm_mix_post,
               norm_ffn_pre, w_up, conv_w, conv_b, w_down, norm_ffn_post)
    params = [_layer_params(l, *weights) for l in range(w_in.shape[0])]

    def trunk(x):
        for p in params:
            x = _layer(x, p)
        return x

    return (trunk(x_prompt), trunk(x_sample))
```

```python
import functools

import numpy as np
import jax
import jax.numpy as jnp
from jax import lax
from jax.experimental import pallas as pl
from jax.experimental.pallas import tpu as pltpu

F32 = jnp.float32
BF16 = jnp.bfloat16

D_MODEL = 1024
HEAD_DIM = 64
ATTN_HEADS = 8
ATTN_WIDTH = ATTN_HEADS * HEAD_DIM
GLA_HEADS = 8
GLA_DK = 32
GLA_DV = 64
GLA_KW = GLA_HEADS * GLA_DK
GLA_VW = GLA_HEADS * GLA_DV
GATE_RANK = 16
GATE_TAU = 16.0
GLA_CHUNK = 32
DIL_PAIRS = ((128, 1), (512, 4), (2048, 16))
N_SIDE = 64
D_FF = 2816
EPS = 1e-6
MASKED_DIST = 1e30

LANES = 128
HEAD_PAIRS = ATTN_HEADS // 2
VMEM_LIMIT = 56 * 1024 * 1024

TM_PROJ = 512
ATT_BQ = 128
ATT_BK = ATT_BQ + 2 * N_SIDE
SEG_PAD = N_SIDE
ATT_UNROLL = 8
GLA_T = 256
FF_CHUNK = 256


def _cparams(*sem):
    return pltpu.CompilerParams(dimension_semantics=sem, vmem_limit_bytes=VMEM_LIMIT)


def _rms(x, g):
    ms = jnp.mean(x * x, axis=-1, keepdims=True)
    return x * lax.rsqrt(ms + EPS) * g


def _const_spec(shape):
    nd = len(shape)
    return pl.BlockSpec(shape, lambda *_: (0,) * nd)


def _inproj_kernel(x_ref, g_ref, wm_ref, wlr_ref, wg_ref, bg_ref,
                   q_ref, k_ref, v_ref, gq_ref, gk_ref, gv_ref, gr_ref, laf_ref, lab_ref):
    h = _rms(x_ref[...], g_ref[...]).astype(BF16)

    def proj(lo, hi):
        return jnp.dot(h, wm_ref[:, lo:hi], preferred_element_type=F32)

    a = ATTN_WIDTH
    q_ref[...] = (proj(0, a) * (HEAD_DIM ** -0.5)).astype(BF16)
    k_ref[...] = proj(a, 2 * a).astype(BF16)
    v_ref[...] = proj(2 * a, 3 * a).astype(BF16)
    o = 3 * a
    gq_ref[...] = proj(o, o + GLA_KW)
    gk_ref[...] = proj(o + GLA_KW, o + 2 * GLA_KW)
    o += 2 * GLA_KW
    gv_ref[...] = proj(o, o + GLA_VW)
    gr_ref[...] = proj(o + GLA_VW, o + 2 * GLA_VW)
    lr = jnp.dot(h, wlr_ref[...], preferred_element_type=F32)
    pre = jnp.dot(lr, wg_ref[...], preferred_element_type=F32,
                  precision=lax.Precision.HIGHEST) + bg_ref[...]
    la = (jnp.minimum(pre, 0.0) - jnp.log(1.0 + jnp.exp(-jnp.abs(pre)))) * (1.0 / GATE_TAU)
    laf_ref[...] = la[:, :GLA_KW]
    lab_ref[...] = la[:, GLA_KW:]


def _inproj(x2, g, wm, wlr, wg, bg):
    n = x2.shape[0]
    tm = TM_PROJ
    row = lambda w: pl.BlockSpec((tm, w), lambda i: (i, 0))
    outs = [(ATTN_WIDTH, BF16)] * 3 + [(GLA_KW, F32)] * 2 + [(GLA_VW, F32)] * 2 + [(GLA_KW, F32)] * 2
    return pl.pallas_call(
        _inproj_kernel,
        grid=(n // tm,),
        in_specs=[row(D_MODEL), _const_spec(g.shape), _const_spec(wm.shape), _const_spec(wlr.shape),
                  _const_spec(wg.shape), _const_spec(bg.shape)],
        out_specs=[row(w) for w, _ in outs],
        out_shape=[jax.ShapeDtypeStruct((n, w), dt) for w, dt in outs],
        compiler_params=_cparams("parallel"),
        name="inproj",
    )(x2, g, wm, wlr, wg, bg)


def _attn_kernel(slope_ref, dist_ref, q_ref, k_ref, v_ref, o_ref,
                 tok_ref, r4_ref, b1_ref, b4_ref, b16_ref, a1o_ref, a1l_ref, a4o_ref, a4l_ref, *, seq):
    s_len = seq
    l4, l16 = s_len // 4, s_len // 16
    p4, p16 = l4 + 2 * SEG_PAD, l16 + 2 * SEG_PAD
    srcs = (q_ref, k_ref, v_ref)
    lane = lax.broadcasted_iota(jnp.int32, (1, LANES), 1)
    first_head = lane < HEAD_DIM
    slope = slope_ref[0]

    zpad = jnp.zeros((SEG_PAD, LANES), BF16)

    def put_segment(buf_ref, a, base, rows, n):
        buf_ref[a, base:base + SEG_PAD, :] = zpad
        buf_ref[a, base + SEG_PAD:base + SEG_PAD + n, :] = rows
        buf_ref[a, base + SEG_PAD + n:base + 2 * SEG_PAD + n, :] = zpad

    for a in range(3):
        def upcast(i, carry, a=a):
            r0 = pl.multiple_of(i * 256, 256)
            tok_ref[a, pl.ds(r0, 256), :] = srcs[a][0, pl.ds(r0, 256), :].astype(F32)
            return carry

        lax.fori_loop(0, s_len // 256, upcast, 0)
        put_segment(b1_ref, a, 0, srcs[a][0], s_len)
        for r in range(4):
            t = tok_ref[a, pl.ds(r, l4, stride=4), :]
            r4_ref[a, r * l4:(r + 1) * l4, :] = t
            put_segment(b4_ref, a, r * p4, t.astype(BF16), l4)
        for r in range(4):
            for sub in range(4):
                t = r4_ref[a, pl.ds(r * l4 + sub, l16, stride=4), :]
                put_segment(b16_ref, a, (r + 4 * sub) * p16, t.astype(BF16), l16)

    def block(buf_ref, seg_base, m0, kind, c):
        krow = pl.multiple_of(seg_base + m0, N_SIDE)
        q = buf_ref[0, pl.ds(pl.multiple_of(krow + SEG_PAD, N_SIDE), ATT_BQ), :]
        k = buf_ref[1, pl.ds(krow, ATT_BK), :]
        v = buf_ref[2, pl.ds(krow, ATT_BK), :]
        dist = dist_ref[kind]
        res = []
        for hh in range(2):
            hm = first_head if hh == 0 else jnp.logical_not(first_head)
            qm = jnp.where(hm, q, jnp.zeros_like(q))
            s = lax.dot_general(qm, k, (((1,), (1,)), ((), ())), preferred_element_type=F32)
            s = s - c[:, hh * HEAD_DIM:hh * HEAD_DIM + 1] * dist
            m = jnp.max(s, axis=-1, keepdims=True)
            p = jnp.exp(s - m)
            den = jnp.sum(p, axis=-1, keepdims=True)
            pv = jnp.dot(p.astype(BF16), v, preferred_element_type=F32)
            res.append((pv / den, m + jnp.log(den)))
        return jnp.where(first_head, res[0][0], res[1][0]), jnp.where(first_head, res[0][1], res[1][1])

    def kind_of(qb, nqb):
        return (qb == 0).astype(jnp.int32) + 2 * (qb == nqb - 1).astype(jnp.int32)

    def merge(oa, la, ob, lb):
        m = jnp.maximum(la, lb)
        ea, eb = jnp.exp(la - m), jnp.exp(lb - m)
        den = ea + eb
        return (ea * oa + eb * ob) / den, m + jnp.log(den)

    n1, n4, n16 = s_len // ATT_BQ, l4 // ATT_BQ, l16 // ATT_BQ

    def branch1(qb, carry):
        m0 = pl.multiple_of(qb * ATT_BQ, ATT_BQ)
        o, l = block(b1_ref, 0, m0, kind_of(qb, n1), slope)
        a1o_ref[pl.ds(m0, ATT_BQ), :] = o
        a1l_ref[pl.ds(m0, ATT_BQ), :] = l
        return carry

    lax.fori_loop(0, n1, branch1, 0, unroll=ATT_UNROLL)

    def branch4(it, carry):
        r, qb = lax.div(it, n4), lax.rem(it, n4)
        m0 = pl.multiple_of(qb * ATT_BQ, ATT_BQ)
        o, l = block(b4_ref, r * p4, m0, kind_of(qb, n4), slope * 4.0)
        row = pl.multiple_of(r * l4 + m0, ATT_BQ)
        a4o_ref[pl.ds(row, ATT_BQ), :] = o
        a4l_ref[pl.ds(row, ATT_BQ), :] = l
        return carry

    lax.fori_loop(0, 4 * n4, branch4, 0, unroll=ATT_UNROLL)

    def branch16(it, carry):
        r, qb = lax.div(it, n16), lax.rem(it, n16)
        m0 = pl.multiple_of(qb * ATT_BQ, ATT_BQ)
        o, l = block(b16_ref, r * p16, m0, kind_of(qb, n16), slope * 16.0)
        rows = pl.ds(lax.rem(r, 4) * l4 + lax.div(r, 4) + 4 * m0, ATT_BQ, stride=4)
        o, l = merge(a4o_ref[rows, :], a4l_ref[rows, :], o, l)
        a4o_ref[rows, :] = o
        a4l_ref[rows, :] = l
        return carry

    lax.fori_loop(0, 16 * n16, branch16, 0, unroll=ATT_UNROLL)

    for r in range(4):
        def final(cb, carry, r=r):
            row = pl.multiple_of(r * l4 + cb * ATT_BQ, ATT_BQ)
            tok = pl.ds(r + 4 * ATT_BQ * cb, ATT_BQ, stride=4)
            o, _ = merge(a1o_ref[tok, :], a1l_ref[tok, :], a4o_ref[pl.ds(row, ATT_BQ), :], a4l_ref[pl.ds(row, ATT_BQ), :])
            o_ref[0, tok, :] = o
            return carry

        lax.fori_loop(0, n4, final, 0)


def _dist_table():
    i = np.arange(ATT_BQ)[:, None]
    j = np.arange(ATT_BK)[None, :]
    delta = j - N_SIDE - i
    ok = np.abs(delta) <= N_SIDE
    first = ok & ((delta >= 0) | (i >= N_SIDE))
    last = ok & ((delta < N_SIDE) | (i < N_SIDE))
    kinds = [ok, first, last, first & last]
    return jnp.asarray(np.stack([np.where(t, np.abs(delta), MASKED_DIST) for t in kinds]).astype(np.float32))


def _attention(q, k, v, slopes):
    b, s, _ = q.shape
    assert s % (16 * ATT_BQ) == 0
    blk = pl.BlockSpec((1, s, LANES), lambda bi, p: (bi, 0, p))
    dist = _dist_table()
    seg_rows = lambda d: d * (s // d + 2 * SEG_PAD)
    return pl.pallas_call(
        functools.partial(_attn_kernel, seq=s),
        grid=(b, HEAD_PAIRS),
        in_specs=[pl.BlockSpec((1, 1, LANES), lambda bi, p: (p, 0, 0)), _const_spec(dist.shape), blk, blk, blk],
        out_specs=blk,
        out_shape=jax.ShapeDtypeStruct((b, s, ATTN_WIDTH), F32),
        scratch_shapes=[pltpu.VMEM((3, s, LANES), F32), pltpu.VMEM((3, s, LANES), F32)]
        + [pltpu.VMEM((3, seg_rows(d), LANES), BF16) for d in (1, 4, 16)]
        + [pltpu.VMEM((s, LANES), F32)] * 4,
        compiler_params=_cparams("parallel", "parallel"),
        name="attn",
    )(slopes, dist, q, k, v)


def _split3(x):
    hi = x.astype(BF16)
    r = x - hi.astype(F32)
    mid = r.astype(BF16)
    lo = (r - mid.astype(F32)).astype(BF16)
    return hi, mid, lo


def _gla_kernel(*refs, reverse):
    if reverse:
        (tri_ref, triT_ref, q_ref, k_ref, v_ref, la_ref, of_ref, r_ref, gn_ref, o_ref, state_ref) = refs
    else:
        (tri_ref, triT_ref, q_ref, k_ref, v_ref, la_ref, o_ref, state_ref) = refs
    T, C, KW = GLA_T, GLA_CHUNK, GLA_KW
    nch = T // C

    @pl.when(pl.program_id(1) == 0)
    def _():
        state_ref[...] = jnp.zeros_like(state_ref)

    la = la_ref[0]
    b = sum(jnp.dot(tri_ref[...], t, preferred_element_type=F32) for t in _split3(la))
    laT = la.T
    bt = sum(jnp.dot(t, triT_ref[...], preferred_element_type=F32) for t in _split3(laT))
    bT, totT = bt[:, :T], bt[:, T:]
    kT = k_ref[0].T
    q_in = q_ref[0] * jnp.exp(b) * (GLA_DK ** -0.5)
    k_intraT = (kT * jnp.exp(-bT)).astype(BF16)
    k_stateT = kT * jnp.exp(totT - bT)
    decayT = jnp.exp(totT)
    v = v_ref[0].astype(BF16)

    lane_k = lax.broadcasted_iota(jnp.int32, (1, KW), 1)
    lane_v = lax.broadcasted_iota(jnp.int32, (1, LANES), 1)
    lane_t = lax.broadcasted_iota(jnp.int32, (1, T), 1)
    ti = lax.broadcasted_iota(jnp.int32, (T, T), 0)
    tj = lax.broadcasted_iota(jnp.int32, (T, T), 1)
    causal = (ti // C == tj // C) & ((tj >= ti) if reverse else (tj <= ti))

    intra = []
    for p in range(GLA_HEADS // 2):
        acc = None
        for hh in range(2):
            h = 2 * p + hh
            qm = jnp.where(lane_k // GLA_DK == h, q_in, 0.0).astype(BF16)
            s = jnp.dot(qm, k_intraT, preferred_element_type=F32)
            att = jnp.where(causal, s, 0.0).astype(BF16)
            pv = jnp.dot(att, v[:, p * LANES:(p + 1) * LANES], preferred_element_type=F32)
            pv = jnp.where(lane_v // GLA_DV == hh, pv, 0.0)
            acc = pv if acc is None else acc + pv
        intra.append(acc)

    par_row = lax.broadcasted_iota(jnp.int32, (2 * GLA_DK, LANES), 0) // GLA_DK
    par_lane = lax.broadcasted_iota(jnp.int32, (2 * GLA_DK, LANES), 1) // GLA_DV
    parity = par_row == par_lane
    u = [[None] * (GLA_HEADS // 2) for _ in range(nch)]
    for p in range(GLA_HEADS // 2):
        rows = k_stateT[p * 2 * GLA_DK:(p + 1) * 2 * GLA_DK, :]
        lhs = jnp.concatenate([jnp.where(lane_t // C == n, rows, 0.0) for n in range(nch)], axis=0).astype(BF16)
        up = jnp.dot(lhs, v[:, p * LANES:(p + 1) * LANES], preferred_element_type=F32)
        for n in range(nch):
            u[n][p] = jnp.where(parity, up[n * 2 * GLA_DK:(n + 1) * 2 * GLA_DK, :], 0.0)

    state = state_ref[...]
    seen = [None] * nch
    order = range(nch - 1, -1, -1) if reverse else range(nch)
    for n in order:
        seen[n] = state.astype(BF16)
        dcol = jnp.broadcast_to(decayT[:, n * C:n * C + 1], (KW, LANES))
        state = dcol * state + jnp.concatenate(u[n], axis=0)
    state_ref[...] = state

    inter = [[None] * nch for _ in range(GLA_HEADS // 2)]
    for n in range(nch):
        qn = q_in[n * C:(n + 1) * C, :]
        lhs = jnp.concatenate([jnp.where(lane_k // (2 * GLA_DK) == p, qn, 0.0) for p in range(GLA_HEADS // 2)],
                              axis=0).astype(BF16)
        res = jnp.dot(lhs, seen[n], preferred_element_type=F32)
        for p in range(GLA_HEADS // 2):
            inter[p][n] = res[p * C:(p + 1) * C, :]
    o = jnp.concatenate([intra[p] + jnp.concatenate(inter[p], axis=0) for p in range(GLA_HEADS // 2)], axis=1)

    if not reverse:
        o_ref[0] = o
        return
    o = o + of_ref[0]
    r = r_ref[0]
    outs = []
    for p in range(GLA_HEADS // 2):
        x = o[:, p * LANES:(p + 1) * LANES]
        sq = x * x
        first = lane_v < GLA_DV
        s0 = jnp.sum(jnp.where(first, sq, 0.0), axis=-1, keepdims=True)
        s1 = jnp.sum(jnp.where(first, 0.0, sq), axis=-1, keepdims=True)
        ms = jnp.where(first, s0, s1) * (1.0 / GLA_DV)
        y = x * lax.rsqrt(ms + EPS) * gn_ref[...]
        rp = r[:, p * LANES:(p + 1) * LANES]
        outs.append(y * rp / (1.0 + jnp.exp(-rp)))
    o_ref[0] = jnp.concatenate(outs, axis=1).astype(o_ref.dtype)


def _gla_consts(reverse):
    i = np.arange(GLA_T)
    same = (i[:, None] // GLA_CHUNK) == (i[None, :] // GLA_CHUNK)
    tri = same & ((i[None, :] >= i[:, None]) if reverse else (i[None, :] <= i[:, None]))
    triT = np.concatenate([tri.T, same], axis=1)
    return jnp.asarray(tri, BF16), jnp.asarray(triT, BF16)


def _gla_dir(gq, gk, gv, la, reverse, o_fwd=None, gr=None, gnorm=None):
    b, s, _ = gq.shape
    nt = s // GLA_T
    tile = (lambda bi, t: (bi, nt - 1 - t, 0)) if reverse else (lambda bi, t: (bi, t, 0))
    kw = pl.BlockSpec((1, GLA_T, GLA_KW), tile)
    vw = pl.BlockSpec((1, GLA_T, GLA_VW), tile)
    tri, triT = _gla_consts(reverse)
    args = [tri, triT, gq, gk, gv, la]
    specs = [_const_spec(tri.shape), _const_spec(triT.shape), kw, kw, vw, kw]
    if reverse:
        args += [o_fwd, gr, gnorm]
        specs += [vw, vw, _const_spec(gnorm.shape)]
    return pl.pallas_call(
        functools.partial(_gla_kernel, reverse=reverse),
        grid=(b, nt),
        in_specs=specs,
        out_specs=vw,
        out_shape=jax.ShapeDtypeStruct((b, s, GLA_VW), BF16 if reverse else F32),
        scratch_shapes=[pltpu.VMEM((GLA_KW, LANES), F32)],
        compiler_params=_cparams("parallel", "arbitrary"),
        name="gla_bwd" if reverse else "gla_fwd",
    )(*args)


def _outproj_kernel(x_ref, oa_ref, og_ref, wa_ref, wg_ref, g_ref, y_ref):
    mix = jnp.dot(oa_ref[...].astype(BF16), wa_ref[...], preferred_element_type=F32)
    mix = mix + jnp.dot(og_ref[...], wg_ref[...], preferred_element_type=F32)
    y_ref[...] = x_ref[...] + _rms(mix, g_ref[...])


def _outproj(x2, oa, og, wa, wg, g):
    n = x2.shape[0]
    tm = TM_PROJ
    row = lambda w: pl.BlockSpec((tm, w), lambda i: (i, 0))
    return pl.pallas_call(
        _outproj_kernel,
        grid=(n // tm,),
        in_specs=[row(D_MODEL), row(ATTN_WIDTH), row(GLA_VW), _const_spec(wa.shape), _const_spec(wg.shape),
                  _const_spec(g.shape)],
        out_specs=row(D_MODEL),
        out_shape=jax.ShapeDtypeStruct((n, D_MODEL), F32),
        compiler_params=_cparams("parallel"),
        name="outproj",
    )(x2, oa, og, wa, wg, g)


def _gelu_tanh(y):
    return 0.5 * y * (1.0 + jnp.tanh(0.7978845608028654 * (y + 0.044715 * (y * y * y))))


def _ffn_kernel(x_ref, xp_ref, xn_ref, g_ref, wup_ref, cw_ref, cb_ref, wdn_ref, gp_ref, y_ref, acc_ref, *, tiles_per_seq):
    tm = x_ref.shape[0]
    i = pl.program_id(0)
    first = (i % tiles_per_seq) == 0
    last = (i % tiles_per_seq) == tiles_per_seq - 1
    x = x_ref[...]
    g = g_ref[...]
    h = _rms(x, g).astype(BF16)
    halo = _rms(jnp.concatenate([xp_ref[...], xn_ref[...]], axis=0), g).astype(BF16)
    row = lax.broadcasted_iota(jnp.int32, (tm, 1), 0)
    for c in range(D_FF // FF_CHUNK):
        lo, hi = c * FF_CHUNK, (c + 1) * FF_CHUNK
        wa = wup_ref[:, lo:hi]
        a = jnp.dot(h, wa, preferred_element_type=F32)
        gate = jnp.dot(h, wup_ref[:, D_FF + lo:D_FF + hi], preferred_element_type=F32)
        ah = jnp.dot(halo, wa, preferred_element_type=F32)
        a_prev = jnp.where(first, 0.0, ah[7:8, :])
        a_next = jnp.where(last, 0.0, ah[8:9, :])
        a_dn = jnp.where(row == 0, a_prev, pltpu.roll(a, 1, axis=0))
        a_up = jnp.where(row == tm - 1, a_next, pltpu.roll(a, tm - 1, axis=0))
        cw = cw_ref[:, lo:hi]
        y = cb_ref[:, lo:hi] + a_dn * cw[0:1, :] + a * cw[1:2, :] + a_up * cw[2:3, :]
        act = (_gelu_tanh(y) * gate).astype(BF16)
        part = jnp.dot(act, wdn_ref[lo:hi, :], preferred_element_type=F32)
        if c == 0:
            acc_ref[...] = part
        else:
            acc_ref[...] += part
    y_ref[...] = x + _rms(acc_ref[...], gp_ref[...])


def _ffn(x2, seq, g, wup, cw, cb, wdn, gp):
    n = x2.shape[0]
    tm = TM_PROJ
    assert seq % tm == 0
    hb = tm // 8
    nblk8 = n // 8
    return pl.pallas_call(
        functools.partial(_ffn_kernel, tiles_per_seq=seq // tm),
        grid=(n // tm,),
        in_specs=[pl.BlockSpec((tm, D_MODEL), lambda i: (i, 0)),
                  pl.BlockSpec((8, D_MODEL), lambda i: (jnp.maximum(i * hb - 1, 0), 0)),
                  pl.BlockSpec((8, D_MODEL), lambda i: (jnp.minimum((i + 1) * hb, nblk8 - 1), 0)),
                  _const_spec(g.shape), _const_spec(wup.shape), _const_spec(cw.shape), _const_spec(cb.shape),
                  _const_spec(wdn.shape), _const_spec(gp.shape)],
        out_specs=pl.BlockSpec((tm, D_MODEL), lambda i: (i, 0)),
        out_shape=jax.ShapeDtypeStruct((n, D_MODEL), F32),
        scratch_shapes=[pltpu.VMEM((tm, D_MODEL), F32)],
        compiler_params=_cparams("parallel"),
        name="ffn",
    )(x2, x2, x2, g, wup, cw, cb, wdn, gp)


def _layer_params(l, norm_mix_pre, w_in, w_gate_fwd, b_gate_fwd, w_gate_bwd, b_gate_bwd, gla_norm, w_out,
                  norm_mix_post, norm_ffn_pre, w_up, conv_w, conv_b, w_down, norm_ffn_post):
    n_main = 3 * ATTN_WIDTH + 2 * GLA_KW + 2 * GLA_VW
    wl = w_in[l]
    row = lambda t: t.reshape(1, -1).astype(F32)
    wlr = jnp.pad(wl[:, n_main:], ((0, 0), (0, LANES - GATE_RANK))).astype(BF16)
    wg = jnp.pad(jnp.concatenate([w_gate_fwd[l], w_gate_bwd[l]], axis=1), ((0, LANES - GATE_RANK), (0, 0)))
    bg = jnp.concatenate([b_gate_fwd[l], b_gate_bwd[l]]).reshape(1, -1)
    slopes = jnp.exp2(-8.0 * jnp.arange(1, ATTN_HEADS + 1, dtype=F32) / ATTN_HEADS)
    return dict(
        n_pre=row(norm_mix_pre[l]), wm=wl[:, :n_main].astype(BF16), wlr=wlr, wg=wg.astype(F32), bg=bg.astype(F32),
        slopes=jnp.repeat(slopes, HEAD_DIM).reshape(HEAD_PAIRS, 1, LANES),
        gnorm=jnp.tile(gla_norm[l].astype(F32), 2).reshape(1, LANES),
        wo_a=w_out[l][:ATTN_WIDTH].astype(BF16), wo_g=w_out[l][ATTN_WIDTH:].astype(BF16), n_post=row(norm_mix_post[l]),
        n_ffn_pre=row(norm_ffn_pre[l]), wup=w_up[l].astype(BF16), cw=conv_w[l].astype(F32), cb=row(conv_b[l]),
        wdn=w_down[l].astype(BF16), n_ffn_post=row(norm_ffn_post[l]),
    )


def _layer(x, p):
    b, s, _ = x.shape
    x2 = x.reshape(b * s, D_MODEL)
    q, k, v, gq, gk, gv, gr, laf, lab = _inproj(x2, p["n_pre"], p["wm"], p["wlr"], p["wg"], p["bg"])
    seq3 = lambda t: t.reshape(b, s, t.shape[-1])
    o_attn = _attention(seq3(q), seq3(k), seq3(v), p["slopes"])
    o_fwd = _gla_dir(seq3(gq), seq3(gk), seq3(gv), seq3(laf), reverse=False)
    o_gla = _gla_dir(seq3(gq), seq3(gk), seq3(gv), seq3(lab), reverse=True, o_fwd=o_fwd, gr=seq3(gr), gnorm=p["gnorm"])
    x2 = _outproj(x2, o_attn.reshape(b * s, ATTN_WIDTH), o_gla.reshape(b * s, GLA_VW), p["wo_a"], p["wo_g"],
                  p["n_post"])
    x2 = _ffn(x2, s, p["n_ffn_pre"], p["wup"], p["cw"], p["cb"], p["wdn"], p["n_ffn_post"])
    return x2.reshape(b, s, D_MODEL)


def kernel(x_prompt, x_sample, norm_mix_pre, w_in, w_gate_fwd, b_gate_fwd, w_gate_bwd, b_gate_bwd, gla_norm, w_out,
           norm_mix_post, norm_ffn_pre, w_up, conv_w, conv_b, w_down, norm_ffn_post):
    weights = (norm_mix_pre, w_in, w_gate_fwd, b_gate_fwd, w_gate_bwd, b_gate_bwd, gla_norm, w_out, norm_mix_post,
               norm_ffn_pre, w_up, conv_w, conv_b, w_down, norm_ffn_post)
    params = [_layer_params(l, *weights) for l in range(w_in.shape[0])]

    def trunk(x):
        for p in params:
            x = _layer(x, p)
        return x

    return (trunk(x_prompt), trunk(x_sample))
```

```python
import functools

import numpy as np
import jax
import jax.numpy as jnp
from jax import lax
from jax.experimental import pallas as pl
from jax.experimental.pallas import tpu as pltpu

F32 = jnp.float32
BF16 = jnp.bfloat16

D_MODEL = 1024
HEAD_DIM = 64
ATTN_HEADS = 8
ATTN_WIDTH = ATTN_HEADS * HEAD_DIM
GLA_HEADS = 8
GLA_DK = 32
GLA_DV = 64
GLA_KW = GLA_HEADS * GLA_DK
GLA_VW = GLA_HEADS * GLA_DV
GATE_RANK = 16
GATE_TAU = 16.0
GLA_CHUNK = 32
DIL_PAIRS = ((128, 1), (512, 4), (2048, 16))
N_SIDE = 64
D_FF = 2816
EPS = 1e-6
MASKED_DIST = 1e30

LANES = 128
HEAD_PAIRS = ATTN_HEADS // 2
VMEM_LIMIT = 56 * 1024 * 1024

TM_PROJ = 512
ATT_BQ = 128
ATT_BK = ATT_BQ + 2 * N_SIDE
SEG_PAD = N_SIDE
ATT_UNROLL = 8
GLA_T = 256
GLA_TILES = 4
FF_CHUNK = 256


def _cparams(*sem):
    return pltpu.CompilerParams(dimension_semantics=sem, vmem_limit_bytes=VMEM_LIMIT)


def _rms(x, g):
    ms = jnp.mean(x * x, axis=-1, keepdims=True)
    return x * lax.rsqrt(ms + EPS) * g


def _const_spec(shape):
    nd = len(shape)
    return pl.BlockSpec(shape, lambda *_: (0,) * nd, pipeline_mode=pl.Buffered(1))


def _inproj_kernel(x_ref, g_ref, wm_ref, wlr_ref, wg_ref, bg_ref,
                   q_ref, k_ref, v_ref, gq_ref, gk_ref, gv_ref, gr_ref, laf_ref, lab_ref):
    h = _rms(x_ref[...], g_ref[...]).astype(BF16)

    def proj(lo, hi):
        return jnp.dot(h, wm_ref[:, lo:hi], preferred_element_type=F32)

    lr = jnp.dot(h, wlr_ref[...], preferred_element_type=F32)
    a = ATTN_WIDTH
    q_ref[...] = (proj(0, a) * (HEAD_DIM ** -0.5)).astype(BF16)
    k_ref[...] = proj(a, 2 * a).astype(BF16)
    v_ref[...] = proj(2 * a, 3 * a).astype(BF16)

    (lr_hi, lr_lo), (wg_hi, wg_lo) = _split2(lr), _split2(wg_ref[...])
    pre = bg_ref[...] + sum(jnp.dot(l, w, preferred_element_type=F32)
                            for l, w in ((lr_hi, wg_hi), (lr_hi, wg_lo), (lr_lo, wg_hi)))
    la = (jnp.minimum(pre, 0.0) - jnp.log(1.0 + jnp.exp(-jnp.abs(pre)))) * (1.0 / GATE_TAU)
    laf_ref[...] = la[:, :GLA_KW]
    lab_ref[...] = la[:, GLA_KW:]

    o = 3 * a
    gq_ref[...] = proj(o, o + GLA_KW)
    gk_ref[...] = proj(o + GLA_KW, o + 2 * GLA_KW)
    o += 2 * GLA_KW
    gv_ref[...] = proj(o, o + GLA_VW)
    gr_ref[...] = proj(o + GLA_VW, o + 2 * GLA_VW)


def _inproj(x2, g, wm, wlr, wg, bg):
    n = x2.shape[0]
    tm = TM_PROJ
    row = lambda w: pl.BlockSpec((tm, w), lambda i: (i, 0))
    outs = [(ATTN_WIDTH, BF16)] * 3 + [(GLA_KW, F32)] * 2 + [(GLA_VW, F32)] * 2 + [(GLA_KW, F32)] * 2
    return pl.pallas_call(
        _inproj_kernel,
        grid=(n // tm,),
        in_specs=[row(D_MODEL), _const_spec(g.shape), _const_spec(wm.shape), _const_spec(wlr.shape),
                  _const_spec(wg.shape), _const_spec(bg.shape)],
        out_specs=[row(w) for w, _ in outs],
        out_shape=[jax.ShapeDtypeStruct((n, w), dt) for w, dt in outs],
        compiler_params=_cparams("parallel"),
        name="inproj",
    )(x2, g, wm, wlr, wg, bg)


def _attn_kernel(slope_ref, dist_ref, q_ref, k_ref, v_ref, o_ref,
                 tok_ref, r4_ref, b1_ref, b4_ref, b16_ref, a1o_ref, a1l_ref, a4o_ref, a4l_ref, *, seq):
    s_len = seq
    l4, l16 = s_len // 4, s_len // 16
    p4, p16 = l4 + 2 * SEG_PAD, l16 + 2 * SEG_PAD
    srcs = (q_ref, k_ref, v_ref)
    lane = lax.broadcasted_iota(jnp.int32, (1, LANES), 1)
    first_head = lane < HEAD_DIM
    slope = slope_ref[0]

    zpad = jnp.zeros((SEG_PAD, LANES), BF16)

    def put_segment(buf_ref, a, base, rows, n):
        buf_ref[a, base:base + SEG_PAD, :] = zpad
        buf_ref[a, base + SEG_PAD:base + SEG_PAD + n, :] = rows
        buf_ref[a, base + SEG_PAD + n:base + 2 * SEG_PAD + n, :] = zpad

    for a in range(3):
        def upcast(i, carry, a=a):
            r0 = pl.multiple_of(i * 256, 256)
            tok_ref[a, pl.ds(r0, 256), :] = srcs[a][0, pl.ds(r0, 256), :].astype(F32)
            return carry

        lax.fori_loop(0, s_len // 256, upcast, 0)
        put_segment(b1_ref, a, 0, srcs[a][0], s_len)
        for r in range(4):
            t = tok_ref[a, pl.ds(r, l4, stride=4), :]
            r4_ref[a, r * l4:(r + 1) * l4, :] = t
            put_segment(b4_ref, a, r * p4, t.astype(BF16), l4)
        for r in range(4):
            for sub in range(4):
                t = r4_ref[a, pl.ds(r * l4 + sub, l16, stride=4), :]
                put_segment(b16_ref, a, (r + 4 * sub) * p16, t.astype(BF16), l16)

    def block(buf_ref, seg_base, m0, kind, c):
        krow = pl.multiple_of(seg_base + m0, N_SIDE)
        q = buf_ref[0, pl.ds(pl.multiple_of(krow + SEG_PAD, N_SIDE), ATT_BQ), :]
        k = buf_ref[1, pl.ds(krow, ATT_BK), :]
        v = buf_ref[2, pl.ds(krow, ATT_BK), :]
        dist = dist_ref[kind]
        res = []
        for hh in range(2):
            hm = first_head if hh == 0 else jnp.logical_not(first_head)
            qm = jnp.where(hm, q, jnp.zeros_like(q))
            s = lax.dot_general(qm, k, (((1,), (1,)), ((), ())), preferred_element_type=F32)
            s = s - c[:, hh * HEAD_DIM:hh * HEAD_DIM + 1] * dist
            m = jnp.max(s, axis=-1, keepdims=True)
            p = jnp.exp(s - m)
            den = jnp.sum(p, axis=-1, keepdims=True)
            pv = jnp.dot(p.astype(BF16), v, preferred_element_type=F32)
            res.append((pv / den, m + jnp.log(den)))
        return jnp.where(first_head, res[0][0], res[1][0]), jnp.where(first_head, res[0][1], res[1][1])

    def kind_of(qb, nqb):
        return (qb == 0).astype(jnp.int32) + 2 * (qb == nqb - 1).astype(jnp.int32)

    def merge(oa, la, ob, lb):
        m = jnp.maximum(la, lb)
        ea, eb = jnp.exp(la - m), jnp.exp(lb - m)
        den = ea + eb
        return (ea * oa + eb * ob) / den, m + jnp.log(den)

    n1, n4, n16 = s_len // ATT_BQ, l4 // ATT_BQ, l16 // ATT_BQ

    def branch1(qb, carry):
        m0 = pl.multiple_of(qb * ATT_BQ, ATT_BQ)
        o, l = block(b1_ref, 0, m0, kind_of(qb, n1), slope)
        a1o_ref[pl.ds(m0, ATT_BQ), :] = o
        a1l_ref[pl.ds(m0, ATT_BQ), :] = l
        return carry

    lax.fori_loop(0, n1, branch1, 0, unroll=ATT_UNROLL)

    def branch4(it, carry):
        r, qb = lax.div(it, n4), lax.rem(it, n4)
        m0 = pl.multiple_of(qb * ATT_BQ, ATT_BQ)
        o, l = block(b4_ref, r * p4, m0, kind_of(qb, n4), slope * 4.0)
        row = pl.multiple_of(r * l4 + m0, ATT_BQ)
        a4o_ref[pl.ds(row, ATT_BQ), :] = o
        a4l_ref[pl.ds(row, ATT_BQ), :] = l
        return carry

    lax.fori_loop(0, 4 * n4, branch4, 0, unroll=ATT_UNROLL)

    def branch16(it, carry):
        r, qb = lax.div(it, n16), lax.rem(it, n16)
        m0 = pl.multiple_of(qb * ATT_BQ, ATT_BQ)
        o, l = block(b16_ref, r * p16, m0, kind_of(qb, n16), slope * 16.0)
        rows = pl.ds(lax.rem(r, 4) * l4 + lax.div(r, 4) + 4 * m0, ATT_BQ, stride=4)
        o, l = merge(a4o_ref[rows, :], a4l_ref[rows, :], o, l)
        a4o_ref[rows, :] = o
        a4l_ref[rows, :] = l
        return carry

    lax.fori_loop(0, 16 * n16, branch16, 0, unroll=ATT_UNROLL)

    for r in range(4):
        def final(cb, carry, r=r):
            row = pl.multiple_of(r * l4 + cb * ATT_BQ, ATT_BQ)
            tok = pl.ds(r + 4 * ATT_BQ * cb, ATT_BQ, stride=4)
            o, _ = merge(a1o_ref[tok, :], a1l_ref[tok, :], a4o_ref[pl.ds(row, ATT_BQ), :], a4l_ref[pl.ds(row, ATT_BQ), :])
            o_ref[0, tok, :] = o
            return carry

        lax.fori_loop(0, n4, final, 0)


def _dist_table():
    i = np.arange(ATT_BQ)[:, None]
    j = np.arange(ATT_BK)[None, :]
    delta = j - N_SIDE - i
    ok = np.abs(delta) <= N_SIDE
    first = ok & ((delta >= 0) | (i >= N_SIDE))
    last = ok & ((delta < N_SIDE) | (i < N_SIDE))
    kinds = [ok, first, last, first & last]
    return jnp.asarray(np.stack([np.where(t, np.abs(delta), MASKED_DIST) for t in kinds]).astype(np.float32))


def _attention(q, k, v, slopes):
    b, s, _ = q.shape
    assert s % (16 * ATT_BQ) == 0
    blk = pl.BlockSpec((1, s, LANES), lambda bi, p: (bi, 0, p))
    dist = _dist_table()
    seg_rows = lambda d: d * (s // d + 2 * SEG_PAD)
    return pl.pallas_call(
        functools.partial(_attn_kernel, seq=s),
        grid=(b, HEAD_PAIRS),
        in_specs=[pl.BlockSpec((1, 1, LANES), lambda bi, p: (p, 0, 0)), _const_spec(dist.shape), blk, blk, blk],
        out_specs=blk,
        out_shape=jax.ShapeDtypeStruct((b, s, ATTN_WIDTH), F32),
        scratch_shapes=[pltpu.VMEM((3, s, LANES), F32), pltpu.VMEM((3, s, LANES), F32)]
        + [pltpu.VMEM((3, seg_rows(d), LANES), BF16) for d in (1, 4, 16)]
        + [pltpu.VMEM((s, LANES), F32)] * 4,
        compiler_params=_cparams("parallel", "parallel"),
        name="attn",
    )(slopes, dist, q, k, v)


def _split2(x):
    hi = x.astype(BF16)
    return hi, (x - hi.astype(F32)).astype(BF16)


def _gla_kernel(*refs, reverse):
    state_ref = refs[-1]

    @pl.when(pl.program_id(1) == 0)
    def _():
        state_ref[...] = jnp.zeros_like(state_ref)

    def view(r, ti):
        return r.at[:, pl.ds(ti * GLA_T, GLA_T), :] if len(r.shape) == 3 else r

    order = range(GLA_TILES - 1, -1, -1) if reverse else range(GLA_TILES)
    tiles = [_gla_tile(*[view(r, ti) for r in refs[:-1]], state_ref, reverse=reverse) for ti in order]
    for _ in range(GLA_PHASES):
        for t in tiles:
            next(t, None)


GLA_PHASES = 3


def _gla_tile(*refs, reverse):
    if reverse:
        (triT_ref, cmask_ref, q_ref, k_ref, v_ref, la_ref, of_ref, r_ref, gn_ref, o_ref, state_ref) = refs
    else:
        (triT_ref, cmask_ref, q_ref, k_ref, v_ref, la_ref, o_ref, state_ref) = refs
    T, C, KW = GLA_T, GLA_CHUNK, GLA_KW
    nch, half, pairs = T // C, T // 2, GLA_HEADS // 2

    laT = la_ref[0].T
    bt = sum(jnp.dot(t, triT_ref[...], preferred_element_type=F32) for t in _split2(laT))
    bT, totT = bt[:, :T], bt[:, T:]
    kT = k_ref[0].T
    q_in = q_ref[0] * jnp.exp(bT.T) * (GLA_DK ** -0.5)
    q_bf = q_in.astype(BF16)
    k_intraT = kT * jnp.exp(-bT)
    k_stateT = (kT * jnp.exp(totT - bT)).astype(BF16)
    decayT = jnp.exp(totT)
    v = v_ref[0].astype(BF16)
    yield

    lane_k =lax.broadcasted_iota(jnp.int32, (1, KW), 1)
    lane_v = lax.broadcasted_iota(jnp.int32, (1, LANES), 1)
    first_half = lane_v < GLA_DV
    row_head = lax.broadcasted_iota(jnp.int32, (KW, 1), 0) // GLA_DK
    si = lax.broadcasted_iota(jnp.int32, (half, T), 0)
    sj = lax.broadcasted_iota(jnp.int32, (half, T), 1) % half
    causal = (si // C == sj // C) & ((sj >= si) if reverse else (sj <= si))

    intra = [[None] * 2 for _ in range(pairs)]
    for t2 in range(2):
        kk = k_intraT[:, t2 * half:(t2 + 1) * half]
        qq = q_bf[t2 * half:(t2 + 1) * half, :]
        for p in range(pairs):
            rhs = jnp.concatenate([jnp.where(row_head == 2 * p, kk, 0.0), jnp.where(row_head == 2 * p + 1, kk, 0.0)],
                                  axis=1).astype(BF16)
            s = jnp.dot(qq, rhs, preferred_element_type=F32)
            att = jnp.where(causal, s, 0.0).astype(BF16)
            vp = v[t2 * half:(t2 + 1) * half, p * LANES:(p + 1) * LANES]
            vbd = jnp.concatenate([jnp.where(first_half, vp, jnp.zeros_like(vp)),
                                   jnp.where(first_half, jnp.zeros_like(vp), vp)], axis=0)
            intra[p][t2] = jnp.dot(att, vbd, preferred_element_type=F32)

    par_row = lax.broadcasted_iota(jnp.int32, (2 * GLA_DK, LANES), 0) // GLA_DK
    par_lane = lax.broadcasted_iota(jnp.int32, (2 * GLA_DK, LANES), 1) // GLA_DV
    parity = par_row == par_lane
    u = [[None] * pairs for _ in range(nch)]
    for p in range(pairs):
        vp = v[:, p * LANES:(p + 1) * LANES]
        vexp = jnp.concatenate([vp] * nch, axis=1) * cmask_ref[...]
        up = jnp.dot(k_stateT[p * 2 * GLA_DK:(p + 1) * 2 * GLA_DK, :], vexp, preferred_element_type=F32)
        for n in range(nch):
            u[n][p] = jnp.where(parity, up[:, n * LANES:(n + 1) * LANES], 0.0)
    yield

    state = state_ref[...]
    seen = [None] * nch
    order = range(nch - 1, -1, -1) if reverse else range(nch)
    for n in order:
        seen[n] = state.astype(BF16)
        dcol = jnp.broadcast_to(decayT[:, n * C:n * C + 1], (KW, LANES))
        state = dcol * state + jnp.concatenate(u[n], axis=0)
    state_ref[...] = state

    inter = [[None] * nch for _ in range(GLA_HEADS // 2)]
    for n in range(nch):
        qn = q_in[n * C:(n + 1) * C, :]
        lhs = jnp.concatenate([jnp.where(lane_k // (2 * GLA_DK) == p, qn, 0.0) for p in range(GLA_HEADS // 2)],
                              axis=0).astype(BF16)
        res = jnp.dot(lhs, seen[n], preferred_element_type=F32)
        for p in range(GLA_HEADS // 2):
            inter[p][n] = res[p * C:(p + 1) * C, :]
    o = jnp.concatenate([jnp.concatenate(intra[p], axis=0) + jnp.concatenate(inter[p], axis=0) for p in range(pairs)],
                        axis=1)

    if not reverse:
        o_ref[0] = o
        return
    o = o + of_ref[0]
    r = r_ref[0]
    outs = []
    for p in range(GLA_HEADS // 2):
        x = o[:, p * LANES:(p + 1) * LANES]
        sq = x * x
        first = lane_v < GLA_DV
        s0 = jnp.sum(jnp.where(first, sq, 0.0), axis=-1, keepdims=True)
        s1 = jnp.sum(jnp.where(first, 0.0, sq), axis=-1, keepdims=True)
        ms = jnp.where(first, s0, s1) * (1.0 / GLA_DV)
        y = x * lax.rsqrt(ms + EPS) * gn_ref[...]
        rp = r[:, p * LANES:(p + 1) * LANES]
        outs.append(y * rp / (1.0 + jnp.exp(-rp)))
    o_ref[0] = jnp.concatenate(outs, axis=1).astype(o_ref.dtype)


def _gla_consts(reverse):
    i = np.arange(GLA_T)
    same = (i[:, None] // GLA_CHUNK) == (i[None, :] // GLA_CHUNK)
    tri = same & ((i[None, :] >= i[:, None]) if reverse else (i[None, :] <= i[:, None]))
    triT = np.concatenate([tri.T, same], axis=1)
    cmask = (i[:, None] // GLA_CHUNK) == (np.arange(GLA_T // GLA_CHUNK * LANES)[None, :] // LANES)
    return jnp.asarray(triT, BF16), jnp.asarray(cmask, BF16)


def _gla_dir(gq, gk, gv, la, reverse, o_fwd=None, gr=None, gnorm=None):
    b, s, _ = gq.shape
    step = GLA_TILES * GLA_T
    assert s % step == 0
    nt = s // step
    tile = (lambda bi, t: (bi, nt - 1 - t, 0)) if reverse else (lambda bi, t: (bi, t, 0))
    kw = pl.BlockSpec((1, step, GLA_KW), tile)
    vw = pl.BlockSpec((1, step, GLA_VW), tile)
    triT, cmask = _gla_consts(reverse)
    args = [triT, cmask, gq, gk, gv, la]
    specs = [_const_spec(triT.shape), _const_spec(cmask.shape), kw, kw, vw, kw]
    if reverse:
        args += [o_fwd, gr, gnorm]
        specs += [vw, vw, _const_spec(gnorm.shape)]
    return pl.pallas_call(
        functools.partial(_gla_kernel, reverse=reverse),
        grid=(b, nt),
        in_specs=specs,
        out_specs=vw,
        out_shape=jax.ShapeDtypeStruct((b, s, GLA_VW), BF16 if reverse else F32),
        scratch_shapes=[pltpu.VMEM((GLA_KW, LANES), F32)],
        compiler_params=_cparams("parallel", "arbitrary"),
        name="gla_bwd" if reverse else "gla_fwd",
    )(*args)


def _outproj_kernel(x_ref, oa_ref, og_ref, wa_ref, wg_ref, g_ref, y_ref):
    mix = jnp.dot(oa_ref[...].astype(BF16), wa_ref[...], preferred_element_type=F32)
    mix = mix + jnp.dot(og_ref[...], wg_ref[...], preferred_element_type=F32)
    y_ref[...] = x_ref[...] + _rms(mix, g_ref[...])


def _outproj(x2, oa, og, wa, wg, g):
    n = x2.shape[0]
    tm = TM_PROJ
    row = lambda w: pl.BlockSpec((tm, w), lambda i: (i, 0))
    return pl.pallas_call(
        _outproj_kernel,
        grid=(n // tm,),
        in_specs=[row(D_MODEL), row(ATTN_WIDTH), row(GLA_VW), _const_spec(wa.shape), _const_spec(wg.shape),
                  _const_spec(g.shape)],
        out_specs=row(D_MODEL),
        out_shape=jax.ShapeDtypeStruct((n, D_MODEL), F32),
        compiler_params=_cparams("parallel"),
        name="outproj",
    )(x2, oa, og, wa, wg, g)


def _gelu_tanh(y):
    return 0.5 * y * (1.0 + jnp.tanh(0.7978845608028654 * (y + 0.044715 * (y * y * y))))


def _ffn_kernel(x_ref, xp_ref, xn_ref, g_ref, wup_ref, cw_ref, cb_ref, wdn_ref, gp_ref, y_ref, act_ref, *, tiles_per_seq):
    tm = x_ref.shape[0]
    i = pl.program_id(0)
    first = (i % tiles_per_seq) == 0
    last = (i % tiles_per_seq) == tiles_per_seq - 1
    x = x_ref[...]
    g = g_ref[...]
    h = _rms(x, g).astype(BF16)
    halo = _rms(jnp.concatenate([xp_ref[...], xn_ref[...]], axis=0), g).astype(BF16)
    row = lax.broadcasted_iota(jnp.int32, (tm, 1), 0)
    for c in range(D_FF // FF_CHUNK):
        lo, hi = c * FF_CHUNK, (c + 1) * FF_CHUNK
        wa = wup_ref[:, lo:hi]
        a = jnp.dot(h, wa, preferred_element_type=F32)
        gate = jnp.dot(h, wup_ref[:, D_FF + lo:D_FF + hi], preferred_element_type=F32)
        ah = jnp.dot(halo, wa, preferred_element_type=F32)
        a_prev = jnp.where(first, 0.0, ah[7:8, :])
        a_next = jnp.where(last, 0.0, ah[8:9, :])
        a_dn = jnp.where(row == 0, a_prev, pltpu.roll(a, 1, axis=0))
        a_up = jnp.where(row == tm - 1, a_next, pltpu.roll(a, tm - 1, axis=0))
        cw = cw_ref[:, lo:hi]
        y = cb_ref[:, lo:hi] + a_dn * cw[0:1, :] + a * cw[1:2, :] + a_up * cw[2:3, :]
        act_ref[:, lo:hi] = (_gelu_tanh(y) * gate).astype(BF16)
    f = jnp.dot(act_ref[...], wdn_ref[...], preferred_element_type=F32)
    y_ref[...] = x + _rms(f, gp_ref[...])


def _ffn(x2, seq, g, wup, cw, cb, wdn, gp):
    n = x2.shape[0]
    tm = TM_PROJ
    assert seq % tm == 0
    hb = tm // 8
    nblk8 = n // 8
    return pl.pallas_call(
        functools.partial(_ffn_kernel, tiles_per_seq=seq // tm),
        grid=(n // tm,),
        in_specs=[pl.BlockSpec((tm, D_MODEL), lambda i: (i, 0)),
                  pl.BlockSpec((8, D_MODEL), lambda i: (jnp.maximum(i * hb - 1, 0), 0)),
                  pl.BlockSpec((8, D_MODEL), lambda i: (jnp.minimum((i + 1) * hb, nblk8 - 1), 0)),
                  _const_spec(g.shape), _const_spec(wup.shape), _const_spec(cw.shape), _const_spec(cb.shape),
                  _const_spec(wdn.shape), _const_spec(gp.shape)],
        out_specs=pl.BlockSpec((tm, D_MODEL), lambda i: (i, 0)),
        out_shape=jax.ShapeDtypeStruct((n, D_MODEL), F32),
        scratch_shapes=[pltpu.VMEM((tm, D_FF), BF16)],
        compiler_params=_cparams("parallel"),
        name="ffn",
    )(x2, x2, x2, g, wup, cw, cb, wdn, gp)


def _layer_params(l, norm_mix_pre, w_in, w_gate_fwd, b_gate_fwd, w_gate_bwd, b_gate_bwd, gla_norm, w_out,
                  norm_mix_post, norm_ffn_pre, w_up, conv_w, conv_b, w_down, norm_ffn_post):
    n_main = 3 * ATTN_WIDTH + 2 * GLA_KW + 2 * GLA_VW
    wl = w_in[l]
    row = lambda t: t.reshape(1, -1).astype(F32)
    wlr = jnp.pad(wl[:, n_main:], ((0, 0), (0, LANES - GATE_RANK))).astype(BF16)
    wg = jnp.pad(jnp.concatenate([w_gate_fwd[l], w_gate_bwd[l]], axis=1), ((0, LANES - GATE_RANK), (0, 0)))
    bg = jnp.concatenate([b_gate_fwd[l], b_gate_bwd[l]]).reshape(1, -1)
    slopes = jnp.exp2(-8.0 * jnp.arange(1, ATTN_HEADS + 1, dtype=F32) / ATTN_HEADS)
    return dict(
        n_pre=row(norm_mix_pre[l]), wm=wl[:, :n_main].astype(BF16), wlr=wlr, wg=wg.astype(F32), bg=bg.astype(F32),
        slopes=jnp.repeat(slopes, HEAD_DIM).reshape(HEAD_PAIRS, 1, LANES),
        gnorm=jnp.tile(gla_norm[l].astype(F32), 2).reshape(1, LANES),
        wo_a=w_out[l][:ATTN_WIDTH].astype(BF16), wo_g=w_out[l][ATTN_WIDTH:].astype(BF16), n_post=row(norm_mix_post[l]),
        n_ffn_pre=row(norm_ffn_pre[l]), wup=w_up[l].astype(BF16), cw=conv_w[l].astype(F32), cb=row(conv_b[l]),
        wdn=w_down[l].astype(BF16), n_ffn_post=row(norm_ffn_post[l]),
    )


def _layer(x, p):
    b, s, _ = x.shape
    x2 = x.reshape(b * s, D_MODEL)
    q, k, v, gq, gk, gv, gr, laf, lab = _inproj(x2, p["n_pre"], p["wm"], p["wlr"], p["wg"], p["bg"])
    seq3 = lambda t: t.reshape(b, s, t.shape[-1])
    o_attn = _attention(seq3(q), seq3(k), seq3(v), p["slopes"])
    o_fwd = _gla_dir(seq3(gq), seq3(gk), seq3(gv), seq3(laf), reverse=False)
    o_gla = _gla_dir(seq3(gq), seq3(gk), seq3(gv), seq3(lab), reverse=True, o_fwd=o_fwd, gr=seq3(gr), gnorm=p["gnorm"])
    x2 = _outproj(x2, o_attn.reshape(b * s, ATTN_WIDTH), o_gla.reshape(b * s, GLA_VW), p["wo_a"], p["wo_g"],
                  p["n_post"])
    x2 = _ffn(x2, s, p["n_ffn_pre"], p["wup"], p["cw"], p["cb"], p["wdn"], p["n_ffn_post"])
    return x2.reshape(b, s, D_MODEL)


def kernel(x_prompt, x_sample, norm_mix_pre, w_in, w_gate_fwd, b_gate_fwd, w_gate_bwd, b_gate_bwd, gla_norm, w_out,
           norm_mix_post, norm_ffn_pre, w_up, conv_w, conv_b, w_down, norm_ffn_post):
    weights = (norm_mix_pre, w_in, w_gate_fwd, b_gate_fwd, w_gate_bwd, b_gate_bwd, gla_norm, w_out, norm_mix_post,
               norm_ffn_pre, w_up, conv_w, conv_b, w_down, norm_ffn_post)
    params = [_layer_params(l, *weights) for l in range(w_in.shape[0])]

    def trunk(x):
        for p in params:
            x = _layer(x, p)
        return x

    return (trunk(x_prompt), trunk(x_sample))
```

```python
import functools

import numpy as np
import jax
import jax.numpy as jnp
from jax import lax
from jax.experimental import pallas as pl
from jax.experimental.pallas import tpu as pltpu

F32 = jnp.float32
BF16 = jnp.bfloat16

D_MODEL = 1024
HEAD_DIM = 64
ATTN_HEADS = 8
ATTN_WIDTH = ATTN_HEADS * HEAD_DIM
GLA_HEADS = 8
GLA_DK = 32
GLA_DV = 64
GLA_KW = GLA_HEADS * GLA_DK
GLA_VW = GLA_HEADS * GLA_DV
GATE_RANK = 16
GATE_TAU = 16.0
GLA_CHUNK = 32
DIL_PAIRS = ((128, 1), (512, 4), (2048, 16))
DILATIONS = tuple(d for _, d in DIL_PAIRS)
N_SIDE = 64
D_FF = 2816
EPS = 1e-6
MASKED_DIST = 1e30

LANES = 128
HEAD_PAIRS = ATTN_HEADS // 2
VMEM_LIMIT = 56 * 1024 * 1024

TM_PROJ = 1024
ATT_BQ = 128
ATT_BK = ATT_BQ + 2 * N_SIDE
SEG_PAD = N_SIDE
ATT_UNROLL = 8
GLA_T = 256
GLA_TILES = 4
FF_CHUNK = 256


def _cparams(*sem):
    return pltpu.CompilerParams(dimension_semantics=sem, vmem_limit_bytes=VMEM_LIMIT)


def _rms(x, g):
    ms = jnp.mean(x * x, axis=-1, keepdims=True)
    return x * lax.rsqrt(ms + EPS) * g


def _const_spec(shape):
    nd = len(shape)
    return pl.BlockSpec(shape, lambda *_: (0,) * nd, pipeline_mode=pl.Buffered(1))


def _inproj_kernel(x_ref, g_ref, wm_ref, wlr_ref, wg_ref, bg_ref,
                   q_ref, k_ref, v_ref, gq_ref, gk_ref, gv_ref, gr_ref, laf_ref, lab_ref):
    h = _rms(x_ref[...], g_ref[...]).astype(BF16)

    def proj(lo, hi):
        return jnp.dot(h, wm_ref[:, lo:hi], preferred_element_type=F32)

    lr = jnp.dot(h, wlr_ref[...], preferred_element_type=F32)
    a = ATTN_WIDTH
    q_ref[...] = (proj(0, a) * (HEAD_DIM ** -0.5)).astype(BF16)
    k_ref[...] = proj(a, 2 * a).astype(BF16)
    v_ref[...] = proj(2 * a, 3 * a).astype(BF16)

    (lr_hi, lr_lo), (wg_hi, wg_lo) = _split2(lr), _split2(wg_ref[...])
    pre = bg_ref[...] + sum(jnp.dot(l, w, preferred_element_type=F32)
                            for l, w in ((lr_hi, wg_hi), (lr_hi, wg_lo), (lr_lo, wg_hi)))
    la = (jnp.minimum(pre, 0.0) - jnp.log(1.0 + jnp.exp(-jnp.abs(pre)))) * (1.0 / GATE_TAU)
    laf_ref[...] = la[:, :GLA_KW]
    lab_ref[...] = la[:, GLA_KW:]

    o = 3 * a
    gq_ref[...] = proj(o, o + GLA_KW)
    gk_ref[...] = proj(o + GLA_KW, o + 2 * GLA_KW)
    o += 2 * GLA_KW
    gv_ref[...] = proj(o, o + GLA_VW)
    gr_ref[...] = proj(o + GLA_VW, o + 2 * GLA_VW)


def _inproj(x2, g, wm, wlr, wg, bg):
    n = x2.shape[0]
    tm = TM_PROJ
    row = lambda w: pl.BlockSpec((tm, w), lambda i: (i, 0))
    outs = [(ATTN_WIDTH, BF16)] * 3 + [(GLA_KW, F32)] * 2 + [(GLA_VW, F32)] * 2 + [(GLA_KW, F32)] * 2
    return pl.pallas_call(
        _inproj_kernel,
        grid=(n // tm,),
        in_specs=[row(D_MODEL), _const_spec(g.shape), _const_spec(wm.shape), _const_spec(wlr.shape),
                  _const_spec(wg.shape), _const_spec(bg.shape)],
        out_specs=[row(w) for w, _ in outs],
        out_shape=[jax.ShapeDtypeStruct((n, w), dt) for w, dt in outs],
        compiler_params=_cparams("parallel"),
        name="inproj",
    )(x2, g, wm, wlr, wg, bg)


def _attn_kernel(slope_ref, dist_ref, q_ref, k_ref, v_ref, o_ref,
                 tok_ref, r4_ref, b1_ref, b4_ref, b16_ref, a1o_ref, a1l_ref, a4o_ref, a4l_ref, bias_ref, *, seq):
    s_len = seq
    l4, l16 = s_len // 4, s_len // 16
    p4, p16 = l4 + 2 * SEG_PAD, l16 + 2 * SEG_PAD
    srcs = (q_ref, k_ref, v_ref)
    lane = lax.broadcasted_iota(jnp.int32, (1, LANES), 1)
    first_head = lane < HEAD_DIM
    slope = slope_ref[0]

    zpad = jnp.zeros((SEG_PAD, LANES), BF16)

    def put_segment(buf_ref, a, base, rows, n):
        buf_ref[a, base:base + SEG_PAD, :] = zpad
        buf_ref[a, base + SEG_PAD:base + SEG_PAD + n, :] = rows
        buf_ref[a, base + SEG_PAD + n:base + 2 * SEG_PAD + n, :] = zpad

    for a in range(3):
        def upcast(i, carry, a=a):
            r0 = pl.multiple_of(i * 256, 256)
            tok_ref[a, pl.ds(r0, 256), :] = srcs[a][0, pl.ds(r0, 256), :].astype(F32)
            return carry

        lax.fori_loop(0, s_len // 256, upcast, 0)
        put_segment(b1_ref, a, 0, srcs[a][0], s_len)
        for r in range(4):
            t = tok_ref[a, pl.ds(r, l4, stride=4), :]
            r4_ref[a, r * l4:(r + 1) * l4, :] = t
            put_segment(b4_ref, a, r * p4, t.astype(BF16), l4)
        for r in range(4):
            for sub in range(4):
                t = r4_ref[a, pl.ds(r * l4 + sub, l16, stride=4), :]
                put_segment(b16_ref, a, (r + 4 * sub) * p16, t.astype(BF16), l16)

    for di, dil in enumerate(DILATIONS):
        for kind in range(dist_ref.shape[0]):
            for hh in range(2):
                c = slope[:, hh * HEAD_DIM:hh * HEAD_DIM + 1] * float(dil)
                bias_ref[di, kind, hh * ATT_BQ:(hh + 1) * ATT_BQ, :] = c * dist_ref[kind]

    def block(buf_ref, seg_base, m0, kind, di):
        krow = pl.multiple_of(seg_base + m0, N_SIDE)
        q = buf_ref[0, pl.ds(pl.multiple_of(krow + SEG_PAD, N_SIDE), ATT_BQ), :]
        k = buf_ref[1, pl.ds(krow, ATT_BK), :]
        v = buf_ref[2, pl.ds(krow, ATT_BK), :]
        zero = jnp.zeros_like(q)
        qs = jnp.concatenate([jnp.where(first_head, q, zero), jnp.where(first_head, zero, q)], axis=0)
        s = lax.dot_general(qs, k, (((1,), (1,)), ((), ())), preferred_element_type=F32) - bias_ref[di, kind]
        m = jnp.max(s, axis=-1, keepdims=True)
        p = jnp.exp(s - m).astype(BF16)
        pv = jnp.dot(p, jnp.concatenate([v, jnp.ones_like(v)], axis=1), preferred_element_type=F32)
        den = pv[:, LANES:]
        o = pv[:, :LANES] / den
        lse = m + jnp.log(den)
        return (jnp.where(first_head, o[:ATT_BQ], o[ATT_BQ:]), jnp.where(first_head, lse[:ATT_BQ], lse[ATT_BQ:]))

    def kind_of(qb, nqb):
        return (qb == 0).astype(jnp.int32) + 2 * (qb == nqb - 1).astype(jnp.int32)

    def merge(oa, la, ob, lb):
        m = jnp.maximum(la, lb)
        ea, eb = jnp.exp(la - m), jnp.exp(lb - m)
        den = ea + eb
        return (ea * oa + eb * ob) / den, m + jnp.log(den)

    n1, n4, n16 = s_len // ATT_BQ, l4 // ATT_BQ, l16 // ATT_BQ

    def branch1(qb, carry):
        m0 = pl.multiple_of(qb * ATT_BQ, ATT_BQ)
        o, l = block(b1_ref, 0, m0, kind_of(qb, n1), 0)
        a1o_ref[pl.ds(m0, ATT_BQ), :] = o
        a1l_ref[pl.ds(m0, ATT_BQ), :] = l
        return carry

    lax.fori_loop(0, n1, branch1, 0, unroll=ATT_UNROLL)

    def branch4(it, carry):
        r, qb = lax.div(it, n4), lax.rem(it, n4)
        m0 = pl.multiple_of(qb * ATT_BQ, ATT_BQ)
        o, l = block(b4_ref, r * p4, m0, kind_of(qb, n4), 1)
        row = pl.multiple_of(r * l4 + m0, ATT_BQ)
        a4o_ref[pl.ds(row, ATT_BQ), :] = o
        a4l_ref[pl.ds(row, ATT_BQ), :] = l
        return carry

    lax.fori_loop(0, 4 * n4, branch4, 0, unroll=ATT_UNROLL)

    def branch16(it, carry):
        r, qb = lax.div(it, n16), lax.rem(it, n16)
        m0 = pl.multiple_of(qb * ATT_BQ, ATT_BQ)
        o, l = block(b16_ref, r * p16, m0, kind_of(qb, n16), 2)
        rows = pl.ds(lax.rem(r, 4) * l4 + lax.div(r, 4) + 4 * m0, ATT_BQ, stride=4)
        o, l = merge(a4o_ref[rows, :], a4l_ref[rows, :], o, l)
        a4o_ref[rows, :] = o
        a4l_ref[rows, :] = l
        return carry

    lax.fori_loop(0, 16 * n16, branch16, 0, unroll=ATT_UNROLL)

    for r in range(4):
        def final(cb, carry, r=r):
            row = pl.multiple_of(r * l4 + cb * ATT_BQ, ATT_BQ)
            tok = pl.ds(r + 4 * ATT_BQ * cb, ATT_BQ, stride=4)
            o, _ = merge(a1o_ref[tok, :], a1l_ref[tok, :], a4o_ref[pl.ds(row, ATT_BQ), :], a4l_ref[pl.ds(row, ATT_BQ), :])
            o_ref[0, tok, :] = o
            return carry

        lax.fori_loop(0, n4, final, 0)


def _dist_table():
    i = np.arange(ATT_BQ)[:, None]
    j = np.arange(ATT_BK)[None, :]
    delta = j - N_SIDE - i
    ok = np.abs(delta) <= N_SIDE
    first = ok & ((delta >= 0) | (i >= N_SIDE))
    last = ok & ((delta < N_SIDE) | (i < N_SIDE))
    kinds = [ok, first, last, first & last]
    return jnp.asarray(np.stack([np.where(t, np.abs(delta), MASKED_DIST) for t in kinds]).astype(np.float32))


def _attention(q, k, v, slopes):
    b, s, _ = q.shape
    assert s % (16 * ATT_BQ) == 0
    blk = pl.BlockSpec((1, s, LANES), lambda bi, p: (bi, 0, p))
    dist = _dist_table()
    seg_rows = lambda d: d * (s // d + 2 * SEG_PAD)
    return pl.pallas_call(
        functools.partial(_attn_kernel, seq=s),
        grid=(b, HEAD_PAIRS),
        in_specs=[pl.BlockSpec((1, 1, LANES), lambda bi, p: (p, 0, 0)), _const_spec(dist.shape), blk, blk, blk],
        out_specs=blk,
        out_shape=jax.ShapeDtypeStruct((b, s, ATTN_WIDTH), F32),
        scratch_shapes=[pltpu.VMEM((3, s, LANES), F32), pltpu.VMEM((3, s, LANES), F32)]
        + [pltpu.VMEM((3, seg_rows(d), LANES), BF16) for d in (1, 4, 16)]
        + [pltpu.VMEM((s, LANES), F32)] * 4
        + [pltpu.VMEM((len(DILATIONS), dist.shape[0], 2 * ATT_BQ, ATT_BK), F32)],
        compiler_params=_cparams("parallel", "parallel"),
        name="attn",
    )(slopes, dist, q, k, v)


def _split2(x):
    hi = x.astype(BF16)
    return hi, (x - hi.astype(F32)).astype(BF16)


def _gla_kernel(*refs, reverse):
    state_ref = refs[-1]

    @pl.when(pl.program_id(1) == 0)
    def _():
        state_ref[...] = jnp.zeros_like(state_ref)

    def view(r, ti):
        return r.at[:, pl.ds(ti * GLA_T, GLA_T), :] if len(r.shape) == 3 else r

    order = range(GLA_TILES - 1, -1, -1) if reverse else range(GLA_TILES)
    tiles = [_gla_tile(*[view(r, ti) for r in refs[:-1]], state_ref, reverse=reverse) for ti in order]
    for _ in range(GLA_PHASES):
        for t in tiles:
            next(t, None)


GLA_PHASES = 3


def _gla_tile(*refs, reverse):
    if reverse:
        (triT_ref, cmask_ref, q_ref, k_ref, v_ref, la_ref, of_ref, r_ref, gn_ref, o_ref, state_ref) = refs
    else:
        (triT_ref, cmask_ref, q_ref, k_ref, v_ref, la_ref, o_ref, state_ref) = refs
    T, C, KW = GLA_T, GLA_CHUNK, GLA_KW
    nch, half, pairs = T // C, T // 2, GLA_HEADS // 2

    laT = la_ref[0].T
    bt = sum(jnp.dot(t, triT_ref[...], preferred_element_type=F32) for t in _split2(laT))
    bT, totT = bt[:, :T], bt[:, T:]
    kT = k_ref[0].T
    q_in = q_ref[0] * jnp.exp(bT.T) * (GLA_DK ** -0.5)
    q_bf = q_in.astype(BF16)
    k_intraT = kT * jnp.exp(-bT)
    k_stateT = (kT * jnp.exp(totT - bT)).astype(BF16)
    decayT = jnp.exp(totT)
    v = v_ref[0].astype(BF16)
    yield

    lane_k =lax.broadcasted_iota(jnp.int32, (1, KW), 1)
    lane_v = lax.broadcasted_iota(jnp.int32, (1, LANES), 1)
    first_half = lane_v < GLA_DV
    row_head = lax.broadcasted_iota(jnp.int32, (KW, 1), 0) // GLA_DK
    si = lax.broadcasted_iota(jnp.int32, (half, T), 0)
    sj = lax.broadcasted_iota(jnp.int32, (half, T), 1) % half
    causal = (si // C == sj // C) & ((sj >= si) if reverse else (sj <= si))

    intra = [[None] * 2 for _ in range(pairs)]
    for t2 in range(2):
        kk = k_intraT[:, t2 * half:(t2 + 1) * half]
        qq = q_bf[t2 * half:(t2 + 1) * half, :]
        for p in range(pairs):
            rhs = jnp.concatenate([jnp.where(row_head == 2 * p, kk, 0.0), jnp.where(row_head == 2 * p + 1, kk, 0.0)],
                                  axis=1).astype(BF16)
            s = jnp.dot(qq, rhs, preferred_element_type=F32)
            att = jnp.where(causal, s, 0.0).astype(BF16)
            vp = v[t2 * half:(t2 + 1) * half, p * LANES:(p + 1) * LANES]
            vbd = jnp.concatenate([jnp.where(first_half, vp, jnp.zeros_like(vp)),
                                   jnp.where(first_half, jnp.zeros_like(vp), vp)], axis=0)
            intra[p][t2] = jnp.dot(att, vbd, preferred_element_type=F32)

    par_row = lax.broadcasted_iota(jnp.int32, (2 * GLA_DK, LANES), 0) // GLA_DK
    par_lane = lax.broadcasted_iota(jnp.int32, (2 * GLA_DK, LANES), 1) // GLA_DV
    parity = par_row == par_lane
    u = [[None] * pairs for _ in range(nch)]
    for p in range(pairs):
        vp = v[:, p * LANES:(p + 1) * LANES]
        vexp = jnp.concatenate([vp] * nch, axis=1) * cmask_ref[...]
        up = jnp.dot(k_stateT[p * 2 * GLA_DK:(p + 1) * 2 * GLA_DK, :], vexp, preferred_element_type=F32)
        for n in range(nch):
            u[n][p] = jnp.where(parity, up[:, n * LANES:(n + 1) * LANES], 0.0)
    yield

    state = state_ref[...]
    seen = [None] * nch
    order = range(nch - 1, -1, -1) if reverse else range(nch)
    for n in order:
        seen[n] = state.astype(BF16)
        dcol = jnp.broadcast_to(decayT[:, n * C:n * C + 1], (KW, LANES))
        state = dcol * state + jnp.concatenate(u[n], axis=0)
    state_ref[...] = state

    inter = [[None] * nch for _ in range(GLA_HEADS // 2)]
    for n in range(nch):
        qn = q_in[n * C:(n + 1) * C, :]
        lhs = jnp.concatenate([jnp.where(lane_k // (2 * GLA_DK) == p, qn, 0.0) for p in range(GLA_HEADS // 2)],
                              axis=0).astype(BF16)
        res = jnp.dot(lhs, seen[n], preferred_element_type=F32)
        for p in range(GLA_HEADS // 2):
            inter[p][n] = res[p * C:(p + 1) * C, :]
    o = jnp.concatenate([jnp.concatenate(intra[p], axis=0) + jnp.concatenate(inter[p], axis=0) for p in range(pairs)],
                        axis=1)

    if not reverse:
        o_ref[0] = o
        return
    o = o + of_ref[0]
    r = r_ref[0]
    outs = []
    for p in range(GLA_HEADS // 2):
        x = o[:, p * LANES:(p + 1) * LANES]
        sq = x * x
        first = lane_v < GLA_DV
        s0 = jnp.sum(jnp.where(first, sq, 0.0), axis=-1, keepdims=True)
        s1 = jnp.sum(jnp.where(first, 0.0, sq), axis=-1, keepdims=True)
        ms = jnp.where(first, s0, s1) * (1.0 / GLA_DV)
        y = x * lax.rsqrt(ms + EPS) * gn_ref[...]
        rp = r[:, p * LANES:(p + 1) * LANES]
        outs.append(y * rp / (1.0 + jnp.exp(-rp)))
    o_ref[0] = jnp.concatenate(outs, axis=1).astype(o_ref.dtype)


def _gla_consts(reverse):
    i = np.arange(GLA_T)
    same = (i[:, None] // GLA_CHUNK) == (i[None, :] // GLA_CHUNK)
    tri = same & ((i[None, :] >= i[:, None]) if reverse else (i[None, :] <= i[:, None]))
    triT = np.concatenate([tri.T, same], axis=1)
    cmask = (i[:, None] // GLA_CHUNK) == (np.arange(GLA_T // GLA_CHUNK * LANES)[None, :] // LANES)
    return jnp.asarray(triT, BF16), jnp.asarray(cmask, BF16)


def _gla_dir(gq, gk, gv, la, reverse, o_fwd=None, gr=None, gnorm=None):
    b, s, _ = gq.shape
    step = GLA_TILES * GLA_T
    assert s % step == 0
    nt = s // step
    tile = (lambda bi, t: (bi, nt - 1 - t, 0)) if reverse else (lambda bi, t: (bi, t, 0))
    kw = pl.BlockSpec((1, step, GLA_KW), tile)
    vw = pl.BlockSpec((1, step, GLA_VW), tile)
    triT, cmask = _gla_consts(reverse)
    args = [triT, cmask, gq, gk, gv, la]
    specs = [_const_spec(triT.shape), _const_spec(cmask.shape), kw, kw, vw, kw]
    if reverse:
        args += [o_fwd, gr, gnorm]
        specs += [vw, vw, _const_spec(gnorm.shape)]
    return pl.pallas_call(
        functools.partial(_gla_kernel, reverse=reverse),
        grid=(b, nt),
        in_specs=specs,
        out_specs=vw,
        out_shape=jax.ShapeDtypeStruct((b, s, GLA_VW), BF16 if reverse else F32),
        scratch_shapes=[pltpu.VMEM((GLA_KW, LANES), F32)],
        compiler_params=_cparams("parallel", "arbitrary"),
        name="gla_bwd" if reverse else "gla_fwd",
    )(*args)


def _outproj_kernel(x_ref, oa_ref, og_ref, wa_ref, wg_ref, g_ref, y_ref):
    mix = jnp.dot(oa_ref[...].astype(BF16), wa_ref[...], preferred_element_type=F32)
    mix = mix + jnp.dot(og_ref[...], wg_ref[...], preferred_element_type=F32)
    y_ref[...] = x_ref[...] + _rms(mix, g_ref[...])


def _outproj(x2, oa, og, wa, wg, g):
    n = x2.shape[0]
    tm = TM_PROJ
    row = lambda w: pl.BlockSpec((tm, w), lambda i: (i, 0))
    return pl.pallas_call(
        _outproj_kernel,
        grid=(n // tm,),
        in_specs=[row(D_MODEL), row(ATTN_WIDTH), row(GLA_VW), _const_spec(wa.shape), _const_spec(wg.shape),
                  _const_spec(g.shape)],
        out_specs=row(D_MODEL),
        out_shape=jax.ShapeDtypeStruct((n, D_MODEL), F32),
        compiler_params=_cparams("parallel"),
        name="outproj",
    )(x2, oa, og, wa, wg, g)


def _gelu_tanh(y):
    return 0.5 * y * (1.0 + jnp.tanh(0.7978845608028654 * (y + 0.044715 * (y * y * y))))


def _ffn_kernel(x_ref, xp_ref, xn_ref, g_ref, wup_ref, cw_ref, cb_ref, wdn_ref, gp_ref, y_ref, act_ref, *, tiles_per_seq):
    tm = x_ref.shape[0]
    i = pl.program_id(0)
    first = (i % tiles_per_seq) == 0
    last = (i % tiles_per_seq) == tiles_per_seq - 1
    x = x_ref[...]
    g = g_ref[...]
    h = _rms(x, g).astype(BF16)
    halo = _rms(jnp.concatenate([xp_ref[...], xn_ref[...]], axis=0), g).astype(BF16)
    row = lax.broadcasted_iota(jnp.int32, (tm, 1), 0)
    for c in range(D_FF // FF_CHUNK):
        lo, hi = c * FF_CHUNK, (c + 1) * FF_CHUNK
        wa = wup_ref[:, lo:hi]
        a = jnp.dot(h, wa, preferred_element_type=F32)
        gate = jnp.dot(h, wup_ref[:, D_FF + lo:D_FF + hi], preferred_element_type=F32)
        ah = jnp.dot(halo, wa, preferred_element_type=F32)
        a_prev = jnp.where(first, 0.0, ah[7:8, :])
        a_next = jnp.where(last, 0.0, ah[8:9, :])
        a_dn = jnp.where(row == 0, a_prev, pltpu.roll(a, 1, axis=0))
        a_up = jnp.where(row == tm - 1, a_next, pltpu.roll(a, tm - 1, axis=0))
        cw = cw_ref[:, lo:hi]
        y = cb_ref[:, lo:hi] + a_dn * cw[0:1, :] + a * cw[1:2, :] + a_up * cw[2:3, :]
        act_ref[:, lo:hi] = (_gelu_tanh(y) * gate).astype(BF16)
    f = jnp.dot(act_ref[...], wdn_ref[...], preferred_element_type=F32)
    y_ref[...] = x + _rms(f, gp_ref[...])


def _ffn(x2, seq, g, wup, cw, cb, wdn, gp):
    n = x2.shape[0]
    tm = TM_PROJ
    assert seq % tm == 0
    hb = tm // 8
    nblk8 = n // 8
    return pl.pallas_call(
        functools.partial(_ffn_kernel, tiles_per_seq=seq // tm),
        grid=(n // tm,),
        in_specs=[pl.BlockSpec((tm, D_MODEL), lambda i: (i, 0)),
                  pl.BlockSpec((8, D_MODEL), lambda i: (jnp.maximum(i * hb - 1, 0), 0)),
                  pl.BlockSpec((8, D_MODEL), lambda i: (jnp.minimum((i + 1) * hb, nblk8 - 1), 0)),
                  _const_spec(g.shape), _const_spec(wup.shape), _const_spec(cw.shape), _const_spec(cb.shape),
                  _const_spec(wdn.shape), _const_spec(gp.shape)],
        out_specs=pl.BlockSpec((tm, D_MODEL), lambda i: (i, 0)),
        out_shape=jax.ShapeDtypeStruct((n, D_MODEL), F32),
        scratch_shapes=[pltpu.VMEM((tm, D_FF), BF16)],
        compiler_params=_cparams("parallel"),
        name="ffn",
    )(x2, x2, x2, g, wup, cw, cb, wdn, gp)


def _layer_params(l,norm_mix_pre, w_in, w_gate_fwd, b_gate_fwd, w_gate_bwd, b_gate_bwd, gla_norm, w_out,
                  norm_mix_post, norm_ffn_pre, w_up, conv_w, conv_b, w_down, norm_ffn_post):
    n_main = 3 * ATTN_WIDTH + 2 * GLA_KW + 2 * GLA_VW
    wl = w_in[l]
    row = lambda t: t.reshape(1, -1).astype(F32)
    wlr = jnp.pad(wl[:, n_main:], ((0, 0), (0, LANES - GATE_RANK))).astype(BF16)
    wg = jnp.pad(jnp.concatenate([w_gate_fwd[l], w_gate_bwd[l]], axis=1), ((0, LANES - GATE_RANK), (0, 0)))
    bg = jnp.concatenate([b_gate_fwd[l], b_gate_bwd[l]]).reshape(1, -1)
    slopes = jnp.exp2(-8.0 * jnp.arange(1, ATTN_HEADS + 1, dtype=F32) / ATTN_HEADS)
    return dict(
        n_pre=row(norm_mix_pre[l]), wm=wl[:, :n_main].astype(BF16), wlr=wlr, wg=wg.astype(F32), bg=bg.astype(F32),
        slopes=jnp.repeat(slopes, HEAD_DIM).reshape(HEAD_PAIRS, 1, LANES),
        gnorm=jnp.tile(gla_norm[l].astype(F32), 2).reshape(1, LANES),
        wo_a=w_out[l][:ATTN_WIDTH].astype(BF16), wo_g=w_out[l][ATTN_WIDTH:].astype(BF16), n_post=row(norm_mix_post[l]),
        n_ffn_pre=row(norm_ffn_pre[l]), wup=w_up[l].astype(BF16), cw=conv_w[l].astype(F32), cb=row(conv_b[l]),
        wdn=w_down[l].astype(BF16), n_ffn_post=row(norm_ffn_post[l]),
    )


def _layer(x, p):
    b, s, _ = x.shape
    x2 = x.reshape(b * s, D_MODEL)
    q, k, v, gq, gk, gv, gr, laf, lab = _inproj(x2, p["n_pre"], p["wm"], p["wlr"], p["wg"], p["bg"])
    seq3 = lambda t: t.reshape(b, s, t.shape[-1])
    o_attn = _attention(seq3(q), seq3(k), seq3(v), p["slopes"])
    o_fwd = _gla_dir(seq3(gq), seq3(gk), seq3(gv), seq3(laf), reverse=False)
    o_gla = _gla_dir(seq3(gq), seq3(gk), seq3(gv), seq3(lab), reverse=True, o_fwd=o_fwd, gr=seq3(gr), gnorm=p["gnorm"])
    x2 = _outproj(x2, o_attn.reshape(b * s, ATTN_WIDTH), o_gla.reshape(b * s, GLA_VW), p["wo_a"], p["wo_g"],
                  p["n_post"])
    x2 = _ffn(x2, s, p["n_ffn_pre"], p["wup"], p["cw"], p["cb"], p["wdn"], p["n_ffn_post"])
    return x2.reshape(b, s, D_MODEL)


def kernel(x_prompt, x_sample, norm_mix_pre, w_in, w_gate_fwd, b_gate_fwd, w_gate_bwd, b_gate_bwd, gla_norm, w_out,
           norm_mix_post, norm_ffn_pre, w_up, conv_w, conv_b, w_down, norm_ffn_post):
    weights = (norm_mix_pre, w_in, w_gate_fwd, b_gate_fwd, w_gate_bwd, b_gate_bwd, gla_norm, w_out, norm_mix_post,
               norm_ffn_pre, w_up, conv_w, conv_b, w_down, norm_ffn_post)
    params = [_layer_params(l, *weights) for l in range(w_in.shape[0])]

    def trunk(x):
        for p in params:
            x = _layer(x, p)
        return x

    return (trunk(x_prompt), trunk(x_sample))
```

```python
import functools

import numpy as np
import jax
import jax.numpy as jnp
from jax import lax
from jax.experimental import pallas as pl
from jax.experimental.pallas import tpu as pltpu

F32 = jnp.float32
BF16 = jnp.bfloat16

D_MODEL = 1024
HEAD_DIM = 64
ATTN_HEADS = 8
ATTN_WIDTH = ATTN_HEADS * HEAD_DIM
GLA_HEADS = 8
GLA_DK = 32
GLA_DV = 64
GLA_KW = GLA_HEADS * GLA_DK
GLA_VW = GLA_HEADS * GLA_DV
GATE_RANK = 16
GATE_TAU = 16.0
GLA_CHUNK = 32
DIL_PAIRS = ((128, 1), (512, 4), (2048, 16))
DILATIONS = tuple(d for _, d in DIL_PAIRS)
N_SIDE = 64
D_FF = 2816
EPS = 1e-6
MASKED_DIST = 1e30

LANES = 128
HEAD_PAIRS = ATTN_HEADS // 2
VMEM_LIMIT = 56 * 1024 * 1024

TM_PROJ = 1024
ATT_BQ = 128
ATT_BK = ATT_BQ + 2 * N_SIDE
SEG_PAD = N_SIDE
ATT_UNROLL = 16
GLA_T = 256
GLA_TILES = 4
FF_CHUNK = 256


def _cparams(*sem):
    return pltpu.CompilerParams(dimension_semantics=sem, vmem_limit_bytes=VMEM_LIMIT)


def _rms(x, g):
    ms = jnp.mean(x * x, axis=-1, keepdims=True)
    return x * lax.rsqrt(ms + EPS) * g


def _const_spec(shape):
    nd = len(shape)
    return pl.BlockSpec(shape, lambda *_: (0,) * nd, pipeline_mode=pl.Buffered(1))


def _inproj_kernel(x_ref, g_ref, wm_ref, wlr_ref, wg_ref, bg_ref,
                   q_ref, k_ref, v_ref, gq_ref, gk_ref, gv_ref, gr_ref, laf_ref, lab_ref):
    h = _rms(x_ref[...], g_ref[...]).astype(BF16)

    def proj(lo, hi):
        return jnp.dot(h, wm_ref[:, lo:hi], preferred_element_type=F32)

    lr = jnp.dot(h, wlr_ref[...], preferred_element_type=F32)
    a = ATTN_WIDTH
    q_ref[...] = (proj(0, a) * (HEAD_DIM ** -0.5)).astype(BF16)
    k_ref[...] = proj(a, 2 * a).astype(BF16)
    v_ref[...] = proj(2 * a, 3 * a).astype(BF16)

    (lr_hi, lr_lo), (wg_hi, wg_lo) = _split2(lr), _split2(wg_ref[...])
    pre = bg_ref[...] + sum(jnp.dot(l, w, preferred_element_type=F32)
                            for l, w in ((lr_hi, wg_hi), (lr_hi, wg_lo), (lr_lo, wg_hi)))
    la = (jnp.minimum(pre, 0.0) - jnp.log(1.0 + jnp.exp(-jnp.abs(pre)))) * (1.0 / GATE_TAU)
    laf_ref[...] = la[:, :GLA_KW]
    lab_ref[...] = la[:, GLA_KW:]

    o = 3 * a
    gq_ref[...] = proj(o, o + GLA_KW)
    gk_ref[...] = proj(o + GLA_KW, o + 2 * GLA_KW)
    o += 2 * GLA_KW
    gv_ref[...] = proj(o, o + GLA_VW)
    gr_ref[...] = proj(o + GLA_VW, o + 2 * GLA_VW)


def _inproj(x2, g, wm, wlr, wg, bg):
    n = x2.shape[0]
    tm = TM_PROJ
    row = lambda w: pl.BlockSpec((tm, w), lambda i: (i, 0))
    outs = [(ATTN_WIDTH, BF16)] * 3 + [(GLA_KW, F32)] * 2 + [(GLA_VW, F32)] * 2 + [(GLA_KW, F32)] * 2
    return pl.pallas_call(
        _inproj_kernel,
        grid=(n // tm,),
        in_specs=[row(D_MODEL), _const_spec(g.shape), _const_spec(wm.shape), _const_spec(wlr.shape),
                  _const_spec(wg.shape), _const_spec(bg.shape)],
        out_specs=[row(w) for w, _ in outs],
        out_shape=[jax.ShapeDtypeStruct((n, w), dt) for w, dt in outs],
        compiler_params=_cparams("parallel"),
        name="inproj",
    )(x2, g, wm, wlr, wg, bg)


def _attn_kernel(slope_ref, dist_ref, q_ref, k_ref, v_ref, o_ref,
                 tok_ref, r4_ref, b1_ref, b4_ref, b16_ref, acc1_ref, acc4_ref, bias_ref, *, seq):
    s_len = seq
    l4, l16 = s_len // 4, s_len // 16
    p4, p16 = l4 + 2 * SEG_PAD, l16 + 2 * SEG_PAD
    srcs = (q_ref, k_ref, v_ref)
    lane = lax.broadcasted_iota(jnp.int32, (1, LANES), 1)
    first_head = lane < HEAD_DIM
    slope = slope_ref[0]

    zpad = jnp.zeros((SEG_PAD, LANES), BF16)

    def put_segment(buf_ref, a, base, rows, n):
        buf_ref[a, base:base + SEG_PAD, :] = zpad
        buf_ref[a, base + SEG_PAD:base + SEG_PAD + n, :] = rows
        buf_ref[a, base + SEG_PAD + n:base + 2 * SEG_PAD + n, :] = zpad

    for a in range(3):
        def upcast(i, carry, a=a):
            r0 = pl.multiple_of(i * 256, 256)
            tok_ref[a, pl.ds(r0, 256), :] = srcs[a][0, pl.ds(r0, 256), :].astype(F32)
            return carry

        lax.fori_loop(0, s_len // 256, upcast, 0)
        put_segment(b1_ref, a, 0, srcs[a][0], s_len)
        for r in range(4):
            t = tok_ref[a, pl.ds(r, l4, stride=4), :]
            r4_ref[a, r * l4:(r + 1) * l4, :] = t
            put_segment(b4_ref, a, r * p4, t.astype(BF16), l4)
        for r in range(4):
            for sub in range(4):
                t = r4_ref[a, pl.ds(r * l4 + sub, l16, stride=4), :]
                put_segment(b16_ref, a, (r + 4 * sub) * p16, t.astype(BF16), l16)

    for di, dil in enumerate(DILATIONS):
        for kind in range(dist_ref.shape[0]):
            for hh in range(2):
                c = slope[:, hh * HEAD_DIM:hh * HEAD_DIM + 1] * float(dil)
                bias_ref[di, kind, hh * ATT_BQ:(hh + 1) * ATT_BQ, :] = c * dist_ref[kind]

    def block(buf_ref, seg_base, m0, kind, di):
        krow = pl.multiple_of(seg_base + m0, N_SIDE)
        q = buf_ref[0, pl.ds(pl.multiple_of(krow + SEG_PAD, N_SIDE), ATT_BQ), :]
        k = buf_ref[1, pl.ds(krow, ATT_BK), :]
        v = buf_ref[2, pl.ds(krow, ATT_BK), :]
        zero = jnp.zeros_like(q)
        qs = jnp.concatenate([jnp.where(first_head, q, zero), jnp.where(first_head, zero, q)], axis=0)
        s = lax.dot_general(qs, k, (((1,), (1,)), ((), ())), preferred_element_type=F32) - bias_ref[di, kind]
        m = jnp.max(s, axis=-1, keepdims=True)
        p = jnp.exp(s - m).astype(BF16)
        pv = jnp.dot(p, jnp.concatenate([v, jnp.ones_like(v)], axis=1), preferred_element_type=F32)
        pick = lambda t: jnp.where(first_head, t[:ATT_BQ], t[ATT_BQ:])
        return pick(pv[:, :LANES]), pick(pv[:, LANES:]), pick(m)

    def kind_of(qb, nqb):
        return jnp.where(qb == 0, 1, 0) + jnp.where(qb == nqb - 1, 2, 0)

    def merge(a, b):
        m = jnp.maximum(a[2], b[2])
        ea, eb = jnp.exp(a[2] - m), jnp.exp(b[2] - m)
        return a[0] * ea + b[0] * eb, a[1] * ea + b[1] * eb, m

    def load(acc_ref, rows):
        return tuple(acc_ref[i, rows, :] for i in range(3))

    def store(acc_ref, rows, part):
        for i in range(3):
            acc_ref[i, rows, :] = part[i]

    n1, n4, n16 = s_len // ATT_BQ, l4 // ATT_BQ, l16 // ATT_BQ

    def branch1(qb, carry):
        m0 = pl.multiple_of(qb * ATT_BQ, ATT_BQ)
        store(acc1_ref, pl.ds(m0, ATT_BQ), block(b1_ref, 0, m0, kind_of(qb, n1), 0))
        return carry

    lax.fori_loop(0, n1, branch1, 0, unroll=ATT_UNROLL)

    def branch4(it, carry):
        r, qb = lax.div(it, n4), lax.rem(it, n4)
        m0 = pl.multiple_of(qb * ATT_BQ, ATT_BQ)
        row = pl.multiple_of(r * l4 + m0, ATT_BQ)
        store(acc4_ref, pl.ds(row, ATT_BQ), block(b4_ref, r * p4, m0, kind_of(qb, n4), 1))
        return carry

    lax.fori_loop(0, 4 * n4, branch4, 0, unroll=ATT_UNROLL)

    def branch16(it, carry):
        r, qb = lax.div(it, n16), lax.rem(it, n16)
        m0 = pl.multiple_of(qb * ATT_BQ, ATT_BQ)
        rows = pl.ds(lax.rem(r, 4) * l4 + lax.div(r, 4) + 4 * m0, ATT_BQ, stride=4)
        store(acc4_ref, rows, merge(load(acc4_ref, rows), block(b16_ref, r * p16, m0, kind_of(qb, n16), 2)))
        return carry

    lax.fori_loop(0, 16 * n16, branch16, 0, unroll=ATT_UNROLL)

    for r in range(4):
        def final(cb, carry, r=r):
            row = pl.multiple_of(r * l4 + cb * ATT_BQ, ATT_BQ)
            tok = pl.ds(r + 4 * ATT_BQ * cb, ATT_BQ, stride=4)
            num, den, _ = merge(load(acc1_ref, tok), load(acc4_ref, pl.ds(row, ATT_BQ)))
            o_ref[0, tok, :] = num / den
            return carry

        lax.fori_loop(0, n4, final, 0)


def _dist_table():
    i = np.arange(ATT_BQ)[:, None]
    j = np.arange(ATT_BK)[None, :]
    delta = j - N_SIDE - i
    ok = np.abs(delta) <= N_SIDE
    first = ok & ((delta >= 0) | (i >= N_SIDE))
    last = ok & ((delta < N_SIDE) | (i < N_SIDE))
    kinds = [ok, first, last, first & last]
    return jnp.asarray(np.stack([np.where(t, np.abs(delta), MASKED_DIST) for t in kinds]).astype(np.float32))


def _attention(q, k, v, slopes):
    b, s, _ = q.shape
    assert s % (16 * ATT_BQ) == 0
    blk = pl.BlockSpec((1, s, LANES), lambda bi, p: (bi, 0, p))
    dist = _dist_table()
    seg_rows = lambda d: d * (s // d + 2 * SEG_PAD)
    return pl.pallas_call(
        functools.partial(_attn_kernel, seq=s),
        grid=(b, HEAD_PAIRS),
        in_specs=[pl.BlockSpec((1, 1, LANES), lambda bi, p: (p, 0, 0)), _const_spec(dist.shape), blk, blk, blk],
        out_specs=blk,
        out_shape=jax.ShapeDtypeStruct((b, s, ATTN_WIDTH), F32),
        scratch_shapes=[pltpu.VMEM((3, s, LANES), F32), pltpu.VMEM((3, s, LANES), F32)]
        + [pltpu.VMEM((3, seg_rows(d), LANES), BF16) for d in (1, 4, 16)]
        + [pltpu.VMEM((3, s, LANES), F32)] * 2
        + [pltpu.VMEM((len(DILATIONS), dist.shape[0], 2 * ATT_BQ, ATT_BK), F32)],
        compiler_params=_cparams("parallel", "parallel"),
        name="attn",
    )(slopes, dist, q, k, v)


def _split2(x):
    hi = x.astype(BF16)
    return hi, (x - hi.astype(F32)).astype(BF16)


def _gla_kernel(*refs, reverse):
    state_ref = refs[-1]

    @pl.when(pl.program_id(1) == 0)
    def _():
        state_ref[...] = jnp.zeros_like(state_ref)

    def view(r, ti):
        return r.at[:, pl.ds(ti * GLA_T, GLA_T), :] if len(r.shape) == 3 else r

    order = range(GLA_TILES - 1, -1, -1) if reverse else range(GLA_TILES)
    tiles = [_gla_tile(*[view(r, ti) for r in refs[:-1]], state_ref, reverse=reverse) for ti in order]
    for _ in range(GLA_PHASES):
        for t in tiles:
            next(t, None)


GLA_PHASES = 3


def _gla_tile(*refs, reverse):
    if reverse:
        (triT_ref, q_ref, k_ref, v_ref, la_ref, of_ref, r_ref, gn_ref, o_ref, state_ref) = refs
    else:
        (triT_ref, q_ref, k_ref, v_ref, la_ref, o_ref, state_ref) = refs
    T, C, KW = GLA_T, GLA_CHUNK, GLA_KW
    nch, half, pairs = T // C, T // 2, GLA_HEADS // 2

    laT = la_ref[0].T
    bt = sum(jnp.dot(t, triT_ref[...], preferred_element_type=F32) for t in _split2(laT))
    bT, totT = bt[:, :T], bt[:, T:]
    kT = k_ref[0].T
    q_in = q_ref[0] * jnp.exp(bT.T) * (GLA_DK ** -0.5)
    q_bf = q_in.astype(BF16)
    k_intraT = kT * jnp.exp(-bT)
    k_stateT = (kT * jnp.exp(totT - bT)).astype(BF16)
    decayT = jnp.exp(totT)
    v = v_ref[0].astype(BF16)
    yield

    lane_k =lax.broadcasted_iota(jnp.int32, (1, KW), 1)
    lane_v = lax.broadcasted_iota(jnp.int32, (1, LANES), 1)
    first_half = lane_v < GLA_DV
    row_head = lax.broadcasted_iota(jnp.int32, (KW, 1), 0) // GLA_DK
    si = lax.broadcasted_iota(jnp.int32, (half, T), 0)
    sj = lax.broadcasted_iota(jnp.int32, (half, T), 1) % half
    causal = (si // C == sj // C) & ((sj >= si) if reverse else (sj <= si))

    intra = [[None] * 2 for _ in range(pairs)]
    for t2 in range(2):
        kk = k_intraT[:, t2 * half:(t2 + 1) * half]
        qq = q_bf[t2 * half:(t2 + 1) * half, :]
        for p in range(pairs):
            rhs = jnp.concatenate([jnp.where(row_head == 2 * p, kk, 0.0), jnp.where(row_head == 2 * p + 1, kk, 0.0)],
                                  axis=1).astype(BF16)
            s = jnp.dot(qq, rhs, preferred_element_type=F32)
            att = jnp.where(causal, s, 0.0).astype(BF16)
            vp = v[t2 * half:(t2 + 1) * half, p * LANES:(p + 1) * LANES]
            vbd = jnp.concatenate([jnp.where(first_half, vp, jnp.zeros_like(vp)),
                                   jnp.where(first_half, jnp.zeros_like(vp), vp)], axis=0)
            intra[p][t2] = jnp.dot(att, vbd, preferred_element_type=F32)

    par_row = lax.broadcasted_iota(jnp.int32, (2 * GLA_DK, LANES), 0) // GLA_DK
    par_lane = lax.broadcasted_iota(jnp.int32, (2 * GLA_DK, LANES), 1) // GLA_DV
    parity = par_row == par_lane
    u = [[None] * pairs for _ in range(nch)]
    for p in range(pairs):
        vp = v[:, p * LANES:(p + 1) * LANES]
        zero = jnp.zeros((C, LANES), BF16)
        vexp = jnp.concatenate(
            [jnp.concatenate([vp[m * C:(m + 1) * C] if m == n else zero for m in range(nch)], axis=0) for n in range(nch)],
            axis=1)
        up = jnp.dot(k_stateT[p * 2 * GLA_DK:(p + 1) * 2 * GLA_DK, :], vexp, preferred_element_type=F32)
        for n in range(nch):
            u[n][p] = jnp.where(parity, up[:, n * LANES:(n + 1) * LANES], 0.0)
    yield

    state = state_ref[...]
    seen = [None] * nch
    order = range(nch - 1, -1, -1) if reverse else range(nch)
    for n in order:
        seen[n] = state.astype(BF16)
        dcol = jnp.broadcast_to(decayT[:, n * C:n * C + 1], (KW, LANES))
        state = dcol * state + jnp.concatenate(u[n], axis=0)
    state_ref[...] = state

    inter = [[None] * nch for _ in range(GLA_HEADS // 2)]
    for n in range(nch):
        qn = q_in[n * C:(n + 1) * C, :]
        lhs = jnp.concatenate([jnp.where(lane_k // (2 * GLA_DK) == p, qn, 0.0) for p in range(GLA_HEADS // 2)],
                              axis=0).astype(BF16)
        res = jnp.dot(lhs, seen[n], preferred_element_type=F32)
        for p in range(GLA_HEADS // 2):
            inter[p][n] = res[p * C:(p + 1) * C, :]
    o = jnp.concatenate([jnp.concatenate(intra[p], axis=0) + jnp.concatenate(inter[p], axis=0) for p in range(pairs)],
                        axis=1)

    if not reverse:
        o_ref[0] = o
        return
    o = o + of_ref[0]
    r = r_ref[0]
    outs = []
    for p in range(GLA_HEADS // 2):
        x = o[:, p * LANES:(p + 1) * LANES]
        sq = x * x
        first = lane_v < GLA_DV
        s0 = jnp.sum(jnp.where(first, sq, 0.0), axis=-1, keepdims=True)
        s1 = jnp.sum(jnp.where(first, 0.0, sq), axis=-1, keepdims=True)
        ms = jnp.where(first, s0, s1) * (1.0 / GLA_DV)
        y = x * lax.rsqrt(ms + EPS) * gn_ref[...]
        rp = r[:, p * LANES:(p + 1) * LANES]
        outs.append(y * rp / (1.0 + jnp.exp(-rp)))
    o_ref[0] = jnp.concatenate(outs, axis=1).astype(o_ref.dtype)


def _gla_consts(reverse):
    i = np.arange(GLA_T)
    same = (i[:, None] // GLA_CHUNK) == (i[None, :] // GLA_CHUNK)
    tri = same & ((i[None, :] >= i[:, None]) if reverse else (i[None, :] <= i[:, None]))
    triT = np.concatenate([tri.T, same], axis=1)
    return jnp.asarray(triT, BF16)


def _gla_dir(gq, gk, gv, la, reverse, o_fwd=None, gr=None, gnorm=None):
    b, s, _ = gq.shape
    step = GLA_TILES * GLA_T
    assert s % step == 0
    nt = s // step
    tile = (lambda bi, t: (bi, nt - 1 - t, 0)) if reverse else (lambda bi, t: (bi, t, 0))
    kw = pl.BlockSpec((1, step, GLA_KW), tile)
    vw = pl.BlockSpec((1, step, GLA_VW), tile)
    triT = _gla_consts(reverse)
    args = [triT, gq, gk, gv, la]
    specs = [_const_spec(triT.shape), kw, kw, vw, kw]
    if reverse:
        args += [o_fwd, gr, gnorm]
        specs += [vw, vw, _const_spec(gnorm.shape)]
    return pl.pallas_call(
        functools.partial(_gla_kernel, reverse=reverse),
        grid=(b, nt),
        in_specs=specs,
        out_specs=vw,
        out_shape=jax.ShapeDtypeStruct((b, s, GLA_VW), BF16 if reverse else F32),
        scratch_shapes=[pltpu.VMEM((GLA_KW, LANES), F32)],
        compiler_params=_cparams("parallel", "arbitrary"),
        name="gla_bwd" if reverse else "gla_fwd",
    )(*args)


def _outproj_kernel(x_ref, oa_ref, og_ref, wa_ref, wg_ref, g_ref, y_ref):
    mix = jnp.dot(oa_ref[...].astype(BF16), wa_ref[...], preferred_element_type=F32)
    mix = mix + jnp.dot(og_ref[...], wg_ref[...], preferred_element_type=F32)
    y_ref[...] = x_ref[...] + _rms(mix, g_ref[...])


def _outproj(x2, oa, og, wa, wg, g):
    n = x2.shape[0]
    tm = TM_PROJ
    row = lambda w: pl.BlockSpec((tm, w), lambda i: (i, 0))
    return pl.pallas_call(
        _outproj_kernel,
        grid=(n // tm,),
        in_specs=[row(D_MODEL), row(ATTN_WIDTH), row(GLA_VW), _const_spec(wa.shape), _const_spec(wg.shape),
                  _const_spec(g.shape)],
        out_specs=row(D_MODEL),
        out_shape=jax.ShapeDtypeStruct((n, D_MODEL), F32),
        compiler_params=_cparams("parallel"),
        name="outproj",
    )(x2, oa, og, wa, wg, g)


def _gelu_tanh(y):
    return 0.5 * y * (1.0 + jnp.tanh(0.7978845608028654 * (y + 0.044715 * (y * y * y))))


def _ffn_kernel(x_ref, xp_ref, xn_ref, g_ref, wup_ref, cw_ref, cb_ref, wdn_ref, gp_ref, y_ref, act_ref, *, tiles_per_seq):
    tm = x_ref.shape[0]
    i = pl.program_id(0)
    first = (i % tiles_per_seq) == 0
    last = (i % tiles_per_seq) == tiles_per_seq - 1
    x = x_ref[...]
    g = g_ref[...]
    h = _rms(x, g).astype(BF16)
    halo = _rms(jnp.concatenate([xp_ref[...], xn_ref[...]], axis=0), g).astype(BF16)
    row = lax.broadcasted_iota(jnp.int32, (tm, 1), 0)
    for c in range(D_FF // FF_CHUNK):
        lo, hi = c * FF_CHUNK, (c + 1) * FF_CHUNK
        wa = wup_ref[:, lo:hi]
        a = jnp.dot(h, wa, preferred_element_type=F32)
        gate = jnp.dot(h, wup_ref[:, D_FF + lo:D_FF + hi], preferred_element_type=F32)
        ah = jnp.dot(halo, wa, preferred_element_type=F32)
        a_prev = jnp.where(first, 0.0, ah[7:8, :])
        a_next = jnp.where(last, 0.0, ah[8:9, :])
        a_dn = jnp.where(row == 0, a_prev, pltpu.roll(a, 1, axis=0))
        a_up = jnp.where(row == tm - 1, a_next, pltpu.roll(a, tm - 1, axis=0))
        cw = cw_ref[:, lo:hi]
        y = cb_ref[:, lo:hi] + a_dn * cw[0:1, :] + a * cw[1:2, :] + a_up * cw[2:3, :]
        act_ref[:, lo:hi] = (_gelu_tanh(y) * gate).astype(BF16)
    f = jnp.dot(act_ref[...], wdn_ref[...], preferred_element_type=F32)
    y_ref[...] = x + _rms(f, gp_ref[...])


def _ffn(x2, seq, g, wup, cw, cb, wdn, gp):
    n = x2.shape[0]
    tm = TM_PROJ
    assert seq % tm == 0
    hb = tm // 8
    nblk8 = n // 8
    return pl.pallas_call(
        functools.partial(_ffn_kernel, tiles_per_seq=seq // tm),
        grid=(n // tm,),
        in_specs=[pl.BlockSpec((tm, D_MODEL), lambda i: (i, 0)),
                  pl.BlockSpec((8, D_MODEL), lambda i: (jnp.maximum(i * hb - 1, 0), 0)),
                  pl.BlockSpec((8, D_MODEL), lambda i: (jnp.minimum((i + 1) * hb, nblk8 - 1), 0)),
                  _const_spec(g.shape), _const_spec(wup.shape), _const_spec(cw.shape), _const_spec(cb.shape),
                  _const_spec(wdn.shape), _const_spec(gp.shape)],
        out_specs=pl.BlockSpec((tm, D_MODEL), lambda i: (i, 0)),
        out_shape=jax.ShapeDtypeStruct((n, D_MODEL), F32),
        scratch_shapes=[pltpu.VMEM((tm, D_FF), BF16)],
        compiler_params=_cparams("parallel"),
        name="ffn",
    )(x2, x2, x2, g, wup, cw, cb, wdn, gp)


def _layer_params(l,norm_mix_pre, w_in, w_gate_fwd, b_gate_fwd, w_gate_bwd, b_gate_bwd, gla_norm, w_out,
                  norm_mix_post, norm_ffn_pre, w_up, conv_w, conv_b, w_down, norm_ffn_post):
    n_main = 3 * ATTN_WIDTH + 2 * GLA_KW + 2 * GLA_VW
    wl = w_in[l]
    row = lambda t: t.reshape(1, -1).astype(F32)
    wlr = jnp.pad(wl[:, n_main:], ((0, 0), (0, LANES - GATE_RANK))).astype(BF16)
    wg = jnp.pad(jnp.concatenate([w_gate_fwd[l], w_gate_bwd[l]], axis=1), ((0, LANES - GATE_RANK), (0, 0)))
    bg = jnp.concatenate([b_gate_fwd[l], b_gate_bwd[l]]).reshape(1, -1)
    slopes = jnp.exp2(-8.0 * jnp.arange(1, ATTN_HEADS + 1, dtype=F32) / ATTN_HEADS)
    return dict(
        n_pre=row(norm_mix_pre[l]), wm=wl[:, :n_main].astype(BF16), wlr=wlr, wg=wg.astype(F32), bg=bg.astype(F32),
        slopes=jnp.repeat(slopes, HEAD_DIM).reshape(HEAD_PAIRS, 1, LANES),
        gnorm=jnp.tile(gla_norm[l].astype(F32), 2).reshape(1, LANES),
        wo_a=w_out[l][:ATTN_WIDTH].astype(BF16), wo_g=w_out[l][ATTN_WIDTH:].astype(BF16), n_post=row(norm_mix_post[l]),
        n_ffn_pre=row(norm_ffn_pre[l]), wup=w_up[l].astype(BF16), cw=conv_w[l].astype(F32), cb=row(conv_b[l]),
        wdn=w_down[l].astype(BF16), n_ffn_post=row(norm_ffn_post[l]),
    )


def _layer(x, p):
    b, s, _ = x.shape
    x2 = x.reshape(b * s, D_MODEL)
    q, k, v, gq, gk, gv, gr, laf, lab = _inproj(x2, p["n_pre"], p["wm"], p["wlr"], p["wg"], p["bg"])
    seq3 = lambda t: t.reshape(b, s, t.shape[-1])
    o_attn = _attention(seq3(q), seq3(k), seq3(v), p["slopes"])
    o_fwd = _gla_dir(seq3(gq), seq3(gk), seq3(gv), seq3(laf), reverse=False)
    o_gla = _gla_dir(seq3(gq), seq3(gk), seq3(gv), seq3(lab), reverse=True, o_fwd=o_fwd, gr=seq3(gr), gnorm=p["gnorm"])
    x2 = _outproj(x2, o_attn.reshape(b * s, ATTN_WIDTH), o_gla.reshape(b * s, GLA_VW), p["wo_a"], p["wo_g"],
                  p["n_post"])
    x2 = _ffn(x2, s, p["n_ffn_pre"], p["wup"], p["cw"], p["cb"], p["wdn"], p["n_ffn_post"])
    return x2.reshape(b, s, D_MODEL)


def kernel(x_prompt, x_sample, norm_mix_pre, w_in, w_gate_fwd, b_gate_fwd, w_gate_bwd, b_gate_bwd, gla_norm, w_out,
           norm_mix_post, norm_ffn_pre, w_up, conv_w, conv_b, w_down, norm_ffn_post):
    weights = (norm_mix_pre, w_in, w_gate_fwd, b_gate_fwd, w_gate_bwd, b_gate_bwd, gla_norm, w_out, norm_mix_post,
               norm_ffn_pre, w_up, conv_w, conv_b, w_down, norm_ffn_post)
    params = [_layer_params(l, *weights) for l in range(w_in.shape[0])]

    def trunk(x):
        for p in params:
            x = _layer(x, p)
        return x

    return (trunk(x_prompt), trunk(x_sample))
```

```python
import functools

import numpy as np
import jax
import jax.numpy as jnp
from jax import lax
from jax.experimental import pallas as pl
from jax.experimental.pallas import tpu as pltpu

F32 = jnp.float32
BF16 = jnp.bfloat16

D_MODEL = 1024
HEAD_DIM = 64
ATTN_HEADS = 8
ATTN_WIDTH = ATTN_HEADS * HEAD_DIM
GLA_HEADS = 8
GLA_DK = 32
GLA_DV = 64
GLA_KW = GLA_HEADS * GLA_DK
GLA_VW = GLA_HEADS * GLA_DV
GATE_RANK = 16
GATE_TAU = 16.0
GLA_CHUNK = 32
DIL_PAIRS = ((128, 1), (512, 4), (2048, 16))
DILATIONS = tuple(d for _, d in DIL_PAIRS)
N_SIDE = 64
D_FF = 2816
EPS = 1e-6
MASKED_DIST = 1e30

LANES = 128
HEAD_PAIRS = ATTN_HEADS // 2
VMEM_LIMIT = 56 * 1024 * 1024

TM_PROJ = 1024
ATT_BQ = 128
ATT_BK = ATT_BQ + 2 * N_SIDE
SEG_PAD = N_SIDE
ATT_UNROLL = 32
GLA_T = 256
GLA_TILES = 4
FF_CHUNK = 256


def _cparams(*sem):
    return pltpu.CompilerParams(dimension_semantics=sem, vmem_limit_bytes=VMEM_LIMIT)


def _rms(x, g):
    ms = jnp.mean(x * x, axis=-1, keepdims=True)
    return x * lax.rsqrt(ms + EPS) * g


def _const_spec(shape):
    nd = len(shape)
    return pl.BlockSpec(shape, lambda *_: (0,) * nd, pipeline_mode=pl.Buffered(1))


def _inproj_kernel(x_ref, g_ref, wm_ref, wlr_ref, wg_ref, bg_ref,
                   q_ref, k_ref, v_ref, gq_ref, gk_ref, gv_ref, gr_ref, laf_ref, lab_ref):
    h = _rms(x_ref[...], g_ref[...]).astype(BF16)

    def proj(lo, hi):
        return jnp.dot(h, wm_ref[:, lo:hi], preferred_element_type=F32)

    lr = jnp.dot(h, wlr_ref[...], preferred_element_type=F32)
    a = ATTN_WIDTH
    q_ref[...] = (proj(0, a) * (HEAD_DIM ** -0.5)).astype(BF16)
    k_ref[...] = proj(a, 2 * a).astype(BF16)
    v_ref[...] = proj(2 * a, 3 * a).astype(BF16)

    (lr_hi, lr_lo), (wg_hi, wg_lo) = _split2(lr), _split2(wg_ref[...])
    pre = bg_ref[...] + sum(jnp.dot(l, w, preferred_element_type=F32)
                            for l, w in ((lr_hi, wg_hi), (lr_hi, wg_lo), (lr_lo, wg_hi)))
    la = (jnp.minimum(pre, 0.0) - jnp.log(1.0 + jnp.exp(-jnp.abs(pre)))) * (1.0 / GATE_TAU)
    laf_ref[...] = la[:, :GLA_KW]
    lab_ref[...] = la[:, GLA_KW:]

    o = 3 * a
    gq_ref[...] = proj(o, o + GLA_KW)
    gk_ref[...] = proj(o + GLA_KW, o + 2 * GLA_KW)
    o += 2 * GLA_KW
    gv_ref[...] = proj(o, o + GLA_VW)
    gr_ref[...] = proj(o + GLA_VW, o + 2 * GLA_VW)


def _inproj(x2, g, wm, wlr, wg, bg):
    n = x2.shape[0]
    tm = TM_PROJ
    row = lambda w: pl.BlockSpec((tm, w), lambda i: (i, 0))
    outs = [(ATTN_WIDTH, BF16)] * 3 + [(GLA_KW, F32)] * 2 + [(GLA_VW, F32)] * 2 + [(GLA_KW, F32)] * 2
    return pl.pallas_call(
        _inproj_kernel,
        grid=(n // tm,),
        in_specs=[row(D_MODEL), _const_spec(g.shape), _const_spec(wm.shape), _const_spec(wlr.shape),
                  _const_spec(wg.shape), _const_spec(bg.shape)],
        out_specs=[row(w) for w, _ in outs],
        out_shape=[jax.ShapeDtypeStruct((n, w), dt) for w, dt in outs],
        compiler_params=_cparams("parallel"),
        name="inproj",
    )(x2, g, wm, wlr, wg, bg)


def _attn_kernel(slope_ref, dist_ref, q_ref, k_ref, v_ref, o_ref,
                 tok_ref, r4_ref, b1_ref, b4_ref, b16_ref, acc1_ref, acc4_ref, bias_ref, *, seq):
    s_len = seq
    l4, l16 = s_len // 4, s_len // 16
    p4, p16 = l4 + 2 * SEG_PAD, l16 + 2 * SEG_PAD
    srcs = (q_ref, k_ref, v_ref)
    lane = lax.broadcasted_iota(jnp.int32, (1, LANES), 1)
    first_head = lane < HEAD_DIM
    slope = slope_ref[0]

    zpad = jnp.zeros((SEG_PAD, LANES), BF16)

    def put_segment(buf_ref, a, base, rows, n):
        buf_ref[a, base:base + SEG_PAD, :] = zpad
        buf_ref[a, base + SEG_PAD:base + SEG_PAD + n, :] = rows
        buf_ref[a, base + SEG_PAD + n:base + 2 * SEG_PAD + n, :] = zpad

    for a in range(3):
        def upcast(i, carry, a=a):
            r0 = pl.multiple_of(i * 256, 256)
            tok_ref[a, pl.ds(r0, 256), :] = srcs[a][0, pl.ds(r0, 256), :].astype(F32)
            return carry

        lax.fori_loop(0, s_len // 256, upcast, 0)
        put_segment(b1_ref, a, 0, srcs[a][0], s_len)
        for r in range(4):
            t = tok_ref[a, pl.ds(r, l4, stride=4), :]
            r4_ref[a, r * l4:(r + 1) * l4, :] = t
            put_segment(b4_ref, a, r * p4, t.astype(BF16), l4)
        for r in range(4):
            for sub in range(4):
                t = r4_ref[a, pl.ds(r * l4 + sub, l16, stride=4), :]
                put_segment(b16_ref, a, (r + 4 * sub) * p16, t.astype(BF16), l16)

    for di, dil in enumerate(DILATIONS):
        for kind in range(dist_ref.shape[0]):
            for hh in range(2):
                c = slope[:, hh * HEAD_DIM:hh * HEAD_DIM + 1] * float(dil)
                bias_ref[di, kind, hh * ATT_BQ:(hh + 1) * ATT_BQ, :] = c * dist_ref[kind]

    def block(buf_ref, seg_base, m0, kind, di):
        krow = pl.multiple_of(seg_base + m0, N_SIDE)
        q = buf_ref[0, pl.ds(pl.multiple_of(krow + SEG_PAD, N_SIDE), ATT_BQ), :]
        k = buf_ref[1, pl.ds(krow, ATT_BK), :]
        v = buf_ref[2, pl.ds(krow, ATT_BK), :]
        zero = jnp.zeros_like(q)
        qs = jnp.concatenate([jnp.where(first_head, q, zero), jnp.where(first_head, zero, q)], axis=0)
        s = lax.dot_general(qs, k, (((1,), (1,)), ((), ())), preferred_element_type=F32) - bias_ref[di, kind]
        m = jnp.max(s, axis=-1, keepdims=True)
        p = jnp.exp(s - m).astype(BF16)
        pv = jnp.dot(p, jnp.concatenate([v, jnp.ones_like(v)], axis=1), preferred_element_type=F32)
        pick = lambda t: jnp.where(first_head, t[:ATT_BQ], t[ATT_BQ:])
        return pick(pv[:, :LANES]), pick(pv[:, LANES:]), pick(m)

    def kind_of(qb, nqb):
        return jnp.where(qb == 0, 1, 0) + jnp.where(qb == nqb - 1, 2, 0)

    def merge(a, b):
        m = jnp.maximum(a[2], b[2])
        ea, eb = jnp.exp(a[2] - m), jnp.exp(b[2] - m)
        return a[0] * ea + b[0] * eb, a[1] * ea + b[1] * eb, m

    def load(acc_ref, rows):
        return tuple(acc_ref[i, rows, :] for i in range(3))

    def store(acc_ref, rows, part):
        for i in range(3):
            acc_ref[i, rows, :] = part[i]

    n1, n4, n16 = s_len // ATT_BQ, l4 // ATT_BQ, l16 // ATT_BQ

    def branch1(qb, carry):
        m0 = pl.multiple_of(qb * ATT_BQ, ATT_BQ)
        store(acc1_ref, pl.ds(m0, ATT_BQ), block(b1_ref, 0, m0, kind_of(qb, n1), 0))
        return carry

    lax.fori_loop(0, n1, branch1, 0, unroll=ATT_UNROLL)

    def branch4(it, carry):
        r, qb = lax.div(it, n4), lax.rem(it, n4)
        m0 = pl.multiple_of(qb * ATT_BQ, ATT_BQ)
        row = pl.multiple_of(r * l4 + m0, ATT_BQ)
        store(acc4_ref, pl.ds(row, ATT_BQ), block(b4_ref, r * p4, m0, kind_of(qb, n4), 1))
        return carry

    lax.fori_loop(0, 4 * n4, branch4, 0, unroll=ATT_UNROLL)

    def branch16(it, carry):
        r, qb = lax.div(it, n16), lax.rem(it, n16)
        m0 = pl.multiple_of(qb * ATT_BQ, ATT_BQ)
        rows = pl.ds(lax.rem(r, 4) * l4 + lax.div(r, 4) + 4 * m0, ATT_BQ, stride=4)
        store(acc4_ref, rows, merge(load(acc4_ref, rows), block(b16_ref, r * p16, m0, kind_of(qb, n16), 2)))
        return carry

    lax.fori_loop(0, 16 * n16, branch16, 0, unroll=ATT_UNROLL)

    for r in range(4):
        def final(cb, carry, r=r):
            row = pl.multiple_of(r * l4 + cb * ATT_BQ, ATT_BQ)
            tok = pl.ds(r + 4 * ATT_BQ * cb, ATT_BQ, stride=4)
            num, den, _ = merge(load(acc1_ref, tok), load(acc4_ref, pl.ds(row, ATT_BQ)))
            o_ref[0, tok, :] = num / den
            return carry

        lax.fori_loop(0, n4, final, 0)


def _dist_table():
    i = np.arange(ATT_BQ)[:, None]
    j = np.arange(ATT_BK)[None, :]
    delta = j - N_SIDE - i
    ok = np.abs(delta) <= N_SIDE
    first = ok & ((delta >= 0) | (i >= N_SIDE))
    last = ok & ((delta < N_SIDE) | (i < N_SIDE))
    kinds = [ok, first, last, first & last]
    return jnp.asarray(np.stack([np.where(t, np.abs(delta), MASKED_DIST) for t in kinds]).astype(np.float32))


def _attention(q, k, v, slopes):
    b, s, _ = q.shape
    assert s % (16 * ATT_BQ) == 0
    blk = pl.BlockSpec((1, s, LANES), lambda bi, p: (bi, 0, p))
    dist = _dist_table()
    seg_rows = lambda d: d * (s // d + 2 * SEG_PAD)
    return pl.pallas_call(
        functools.partial(_attn_kernel, seq=s),
        grid=(b, HEAD_PAIRS),
        in_specs=[pl.BlockSpec((1, 1, LANES), lambda bi, p: (p, 0, 0)), _const_spec(dist.shape), blk, blk, blk],
        out_specs=blk,
        out_shape=jax.ShapeDtypeStruct((b, s, ATTN_WIDTH), F32),
        scratch_shapes=[pltpu.VMEM((3, s, LANES), F32), pltpu.VMEM((3, s, LANES), F32)]
        + [pltpu.VMEM((3, seg_rows(d), LANES), BF16) for d in (1, 4, 16)]
        + [pltpu.VMEM((3, s, LANES), F32)] * 2
        + [pltpu.VMEM((len(DILATIONS), dist.shape[0], 2 * ATT_BQ, ATT_BK), F32)],
        compiler_params=_cparams("parallel", "parallel"),
        name="attn",
    )(slopes, dist, q, k, v)


def _split2(x):
    hi = x.astype(BF16)
    return hi, (x - hi.astype(F32)).astype(BF16)


def _gla_kernel(*refs, reverse):
    state_ref = refs[-1]

    @pl.when(pl.program_id(1) == 0)
    def _():
        state_ref[...] = jnp.zeros_like(state_ref)

    def view(r, ti):
        return r.at[:, pl.ds(ti * GLA_T, GLA_T), :] if len(r.shape) == 3 else r

    order = range(GLA_TILES - 1, -1, -1) if reverse else range(GLA_TILES)
    tiles = [_gla_tile(*[view(r, ti) for r in refs[:-1]], state_ref, reverse=reverse) for ti in order]
    for _ in range(GLA_PHASES):
        for t in tiles:
            next(t, None)


GLA_PHASES = 3


def _gla_tile(*refs, reverse):
    if reverse:
        (triT_ref, q_ref, k_ref, v_ref, la_ref, of_ref, r_ref, gn_ref, o_ref, state_ref) = refs
    else:
        (triT_ref, q_ref, k_ref, v_ref, la_ref, o_ref, state_ref) = refs
    T, C, KW = GLA_T, GLA_CHUNK, GLA_KW
    nch, half, pairs = T // C, T // 2, GLA_HEADS // 2

    laT = la_ref[0].T
    bt = sum(jnp.dot(t, triT_ref[...], preferred_element_type=F32) for t in _split2(laT))
    bT, totT = bt[:, :T], bt[:, T:]
    kT = k_ref[0].T
    q_in = q_ref[0] * jnp.exp(bT.T) * (GLA_DK ** -0.5)
    q_bf = q_in.astype(BF16)
    k_intraT = kT * jnp.exp(-bT)
    k_stateT = (kT * jnp.exp(totT - bT)).astype(BF16)
    decayT = jnp.exp(totT)
    v = v_ref[0].astype(BF16)
    yield

    lane_k =lax.broadcasted_iota(jnp.int32, (1, KW), 1)
    lane_v = lax.broadcasted_iota(jnp.int32, (1, LANES), 1)
    first_half = lane_v < GLA_DV
    row_head = lax.broadcasted_iota(jnp.int32, (KW, 1), 0) // GLA_DK
    si = lax.broadcasted_iota(jnp.int32, (half, T), 0)
    sj = lax.broadcasted_iota(jnp.int32, (half, T), 1) % half
    causal = (si // C == sj // C) & ((sj >= si) if reverse else (sj <= si))

    intra = [[None] * 2 for _ in range(pairs)]
    for t2 in range(2):
        kk = k_intraT[:, t2 * half:(t2 + 1) * half]
        qq = q_bf[t2 * half:(t2 + 1) * half, :]
        for p in range(pairs):
            rhs = jnp.concatenate([jnp.where(row_head == 2 * p, kk, 0.0), jnp.where(row_head == 2 * p + 1, kk, 0.0)],
                                  axis=1).astype(BF16)
            s = jnp.dot(qq, rhs, preferred_element_type=F32)
            att = jnp.where(causal, s, 0.0).astype(BF16)
            vp = v[t2 * half:(t2 + 1) * half, p * LANES:(p + 1) * LANES]
            vbd = jnp.concatenate([jnp.where(first_half, vp, jnp.zeros_like(vp)),
                                   jnp.where(first_half, jnp.zeros_like(vp), vp)], axis=0)
            intra[p][t2] = jnp.dot(att, vbd, preferred_element_type=F32)

    par_row = lax.broadcasted_iota(jnp.int32, (2 * GLA_DK, LANES), 0) // GLA_DK
    par_lane = lax.broadcasted_iota(jnp.int32, (2 * GLA_DK, LANES), 1) // GLA_DV
    parity = par_row == par_lane
    u = [[None] * pairs for _ in range(nch)]
    for p in range(pairs):
        vp = v[:, p * LANES:(p + 1) * LANES]
        zero = jnp.zeros((C, LANES), BF16)
        vexp = jnp.concatenate(
            [jnp.concatenate([vp[m * C:(m + 1) * C] if m == n else zero for m in range(nch)], axis=0) for n in range(nch)],
            axis=1)
        up = jnp.dot(k_stateT[p * 2 * GLA_DK:(p + 1) * 2 * GLA_DK, :], vexp, preferred_element_type=F32)
        for n in range(nch):
            u[n][p] = jnp.where(parity, up[:, n * LANES:(n + 1) * LANES], 0.0)
    yield

    state = state_ref[...]
    seen = [None] * nch
    order = range(nch - 1, -1, -1) if reverse else range(nch)
    for n in order:
        seen[n] = state.astype(BF16)
        dcol = jnp.broadcast_to(decayT[:, n * C:n * C + 1], (KW, LANES))
        state = dcol * state + jnp.concatenate(u[n], axis=0)
    state_ref[...] = state

    inter = [[None] * nch for _ in range(GLA_HEADS // 2)]
    for n in range(nch):
        qn = q_in[n * C:(n + 1) * C, :]
        lhs = jnp.concatenate([jnp.where(lane_k // (2 * GLA_DK) == p, qn, 0.0) for p in range(GLA_HEADS // 2)],
                              axis=0).astype(BF16)
        res = jnp.dot(lhs, seen[n], preferred_element_type=F32)
        for p in range(GLA_HEADS // 2):
            inter[p][n] = res[p * C:(p + 1) * C, :]
    o = jnp.concatenate([jnp.concatenate(intra[p], axis=0) + jnp.concatenate(inter[p], axis=0) for p in range(pairs)],
                        axis=1)

    if not reverse:
        o_ref[0] = o
        return
    o = o + of_ref[0]
    r = r_ref[0]
    outs = []
    for p in range(GLA_HEADS // 2):
        x = o[:, p * LANES:(p + 1) * LANES]
        sq = x * x
        first = lane_v < GLA_DV
        s0 = jnp.sum(jnp.where(first, sq, 0.0), axis=-1, keepdims=True)
        s1 = jnp.sum(jnp.where(first, 0.0, sq), axis=-1, keepdims=True)
        ms = jnp.where(first, s0, s1) * (1.0 / GLA_DV)
        y = x * lax.rsqrt(ms + EPS) * gn_ref[...]
        rp = r[:, p * LANES:(p + 1) * LANES]
        outs.append(y * rp / (1.0 + jnp.exp(-rp)))
    o_ref[0] = jnp.concatenate(outs, axis=1).astype(o_ref.dtype)


def _gla_consts(reverse):
    i = np.arange(GLA_T)
    same = (i[:, None] // GLA_CHUNK) == (i[None, :] // GLA_CHUNK)
    tri = same & ((i[None, :] >= i[:, None]) if reverse else (i[None, :] <= i[:, None]))
    triT = np.concatenate([tri.T, same], axis=1)
    return jnp.asarray(triT, BF16)


def _gla_dir(gq, gk, gv, la, reverse, o_fwd=None, gr=None, gnorm=None):
    b, s, _ = gq.shape
    step = GLA_TILES * GLA_T
    assert s % step == 0
    nt = s // step
    tile = (lambda bi, t: (bi, nt - 1 - t, 0)) if reverse else (lambda bi, t: (bi, t, 0))
    kw = pl.BlockSpec((1, step, GLA_KW), tile)
    vw = pl.BlockSpec((1, step, GLA_VW), tile)
    triT = _gla_consts(reverse)
    args = [triT, gq, gk, gv, la]
    specs = [_const_spec(triT.shape), kw, kw, vw, kw]
    if reverse:
        args += [o_fwd, gr, gnorm]
        specs += [vw, vw, _const_spec(gnorm.shape)]
    return pl.pallas_call(
        functools.partial(_gla_kernel, reverse=reverse),
        grid=(b, nt),
        in_specs=specs,
        out_specs=vw,
        out_shape=jax.ShapeDtypeStruct((b, s, GLA_VW), BF16 if reverse else F32),
        scratch_shapes=[pltpu.VMEM((GLA_KW, LANES), F32)],
        compiler_params=_cparams("parallel", "arbitrary"),
        name="gla_bwd" if reverse else "gla_fwd",
    )(*args)


def _mix_residual(x, oa, og, wa_ref, wg_ref, g):
    mix = jnp.dot(oa.astype(BF16), wa_ref[...], preferred_element_type=F32)
    mix = mix + jnp.dot(og, wg_ref[...], preferred_element_type=F32)
    return x + _rms(mix, g)


def _gelu_tanh(y):
    return 0.5 * y * (1.0 + jnp.tanh(0.7978845608028654 * (y + 0.044715 * (y * y * y))))


def _ffn_kernel(x_ref, xp_ref, xn_ref, oa_ref, oap_ref, oan_ref, og_ref, ogp_ref, ogn_ref, wa_ref, wg_ref, gm_ref,
                g_ref, wup_ref, cw_ref, cb_ref, wdn_ref, gp_ref, y_ref, act_ref, *, tiles_per_seq):
    tm = x_ref.shape[0]
    i = pl.program_id(0)
    first = (i % tiles_per_seq) == 0
    last = (i % tiles_per_seq) == tiles_per_seq - 1
    gm = gm_ref[...]
    x = _mix_residual(x_ref[...], oa_ref[...], og_ref[...], wa_ref, wg_ref, gm)
    og_halo = jnp.concatenate([ogp_ref[...].astype(F32)[8:], ogn_ref[...].astype(F32)[:8]], axis=0)
    x_halo = _mix_residual(jnp.concatenate([xp_ref[...], xn_ref[...]], axis=0),
                           jnp.concatenate([oap_ref[...], oan_ref[...]], axis=0), og_halo.astype(BF16), wa_ref, wg_ref, gm)
    g = g_ref[...]
    h = _rms(x, g).astype(BF16)
    halo = _rms(x_halo, g).astype(BF16)
    row = lax.broadcasted_iota(jnp.int32, (tm, 1), 0)
    for lo in range(0, D_FF, FF_CHUNK):
        hi = min(lo + FF_CHUNK, D_FF)
        wa = wup_ref[:, lo:hi]
        a = jnp.dot(h, wa, preferred_element_type=F32)
        gate = jnp.dot(h, wup_ref[:, D_FF + lo:D_FF + hi], preferred_element_type=F32)
        ah = jnp.dot(halo, wa, preferred_element_type=F32)
        a_prev = jnp.where(first, 0.0, ah[7:8, :])
        a_next = jnp.where(last, 0.0, ah[8:9, :])
        a_dn = jnp.where(row == 0, a_prev, pltpu.roll(a, 1, axis=0))
        a_up = jnp.where(row == tm - 1, a_next, pltpu.roll(a, tm - 1, axis=0))
        cw = cw_ref[:, lo:hi]
        y = cb_ref[:, lo:hi] + a_dn * cw[0:1, :] + a * cw[1:2, :] + a_up * cw[2:3, :]
        act_ref[:, lo:hi] = (_gelu_tanh(y) * gate).astype(BF16)
    f = jnp.dot(act_ref[...], wdn_ref[...], preferred_element_type=F32)
    y_ref[...] = x + _rms(f, gp_ref[...])


def _ffn(x2, oa, og, seq, wa, wg, gm, g, wup, cw, cb, wdn, gp):
    n = x2.shape[0]
    tm = TM_PROJ
    assert seq % tm == 0

    def tile_and_halo(width, rows):
        per, nblk = tm // rows, n // rows
        return [pl.BlockSpec((tm, width), lambda i: (i, 0)),
                pl.BlockSpec((rows, width), lambda i: (jnp.maximum(i * per - 1, 0), 0)),
                pl.BlockSpec((rows, width), lambda i: (jnp.minimum((i + 1) * per, nblk - 1), 0))]

    consts = (wa, wg, gm, g, wup, cw, cb, wdn, gp)
    return pl.pallas_call(
        functools.partial(_ffn_kernel, tiles_per_seq=seq // tm),
        grid=(n // tm,),
        in_specs=tile_and_halo(D_MODEL, 8) + tile_and_halo(ATTN_WIDTH, 8) + tile_and_halo(GLA_VW, 16)
        + [_const_spec(c.shape) for c in consts],
        out_specs=pl.BlockSpec((tm, D_MODEL), lambda i: (i, 0)),
        out_shape=jax.ShapeDtypeStruct((n, D_MODEL), F32),
        scratch_shapes=[pltpu.VMEM((tm, D_FF), BF16)],
        compiler_params=_cparams("parallel"),
        name="ffn",
    )(x2, x2, x2, oa, oa, oa, og, og, og, *consts)


def _layer_params(l,norm_mix_pre, w_in, w_gate_fwd, b_gate_fwd, w_gate_bwd, b_gate_bwd, gla_norm, w_out,
                  norm_mix_post, norm_ffn_pre, w_up, conv_w, conv_b, w_down, norm_ffn_post):
    n_main = 3 * ATTN_WIDTH + 2 * GLA_KW + 2 * GLA_VW
    wl = w_in[l]
    row = lambda t: t.reshape(1, -1).astype(F32)
    wlr = jnp.pad(wl[:, n_main:], ((0, 0), (0, LANES - GATE_RANK))).astype(BF16)
    wg = jnp.pad(jnp.concatenate([w_gate_fwd[l], w_gate_bwd[l]], axis=1), ((0, LANES - GATE_RANK), (0, 0)))
    bg = jnp.concatenate([b_gate_fwd[l], b_gate_bwd[l]]).reshape(1, -1)
    slopes = jnp.exp2(-8.0 * jnp.arange(1, ATTN_HEADS + 1, dtype=F32) / ATTN_HEADS)
    return dict(
        n_pre=row(norm_mix_pre[l]), wm=wl[:, :n_main].astype(BF16), wlr=wlr, wg=wg.astype(F32), bg=bg.astype(F32),
        slopes=jnp.repeat(slopes, HEAD_DIM).reshape(HEAD_PAIRS, 1, LANES),
        gnorm=jnp.tile(gla_norm[l].astype(F32), 2).reshape(1, LANES),
        wo_a=w_out[l][:ATTN_WIDTH].astype(BF16), wo_g=w_out[l][ATTN_WIDTH:].astype(BF16), n_post=row(norm_mix_post[l]),
        n_ffn_pre=row(norm_ffn_pre[l]), wup=w_up[l].astype(BF16), cw=conv_w[l].astype(F32), cb=row(conv_b[l]),
        wdn=w_down[l].astype(BF16), n_ffn_post=row(norm_ffn_post[l]),
    )


def _layer(x, p):
    b, s, _ = x.shape
    x2 = x.reshape(b * s, D_MODEL)
    q, k, v, gq, gk, gv, gr, laf, lab = _inproj(x2, p["n_pre"], p["wm"], p["wlr"], p["wg"], p["bg"])
    seq3 = lambda t: t.reshape(b, s, t.shape[-1])
    o_attn = _attention(seq3(q), seq3(k), seq3(v), p["slopes"])
    o_fwd = _gla_dir(seq3(gq), seq3(gk), seq3(gv), seq3(laf), reverse=False)
    o_gla = _gla_dir(seq3(gq), seq3(gk), seq3(gv), seq3(lab), reverse=True, o_fwd=o_fwd, gr=seq3(gr), gnorm=p["gnorm"])
    x2 = _ffn(x2, o_attn.reshape(b * s, ATTN_WIDTH), o_gla.reshape(b * s, GLA_VW), s, p["wo_a"], p["wo_g"], p["n_post"],
              p["n_ffn_pre"], p["wup"], p["cw"], p["cb"], p["wdn"], p["n_ffn_post"])
    return x2.reshape(b, s, D_MODEL)


def kernel(x_prompt, x_sample, norm_mix_pre, w_in, w_gate_fwd, b_gate_fwd, w_gate_bwd, b_gate_bwd, gla_norm, w_out,
           norm_mix_post, norm_ffn_pre, w_up, conv_w, conv_b, w_down, norm_ffn_post):
    weights = (norm_mix_pre, w_in, w_gate_fwd, b_gate_fwd, w_gate_bwd, b_gate_bwd, gla_norm, w_out, norm_mix_post,
               norm_ffn_pre, w_up, conv_w, conv_b, w_down, norm_ffn_post)
    params = [_layer_params(l, *weights) for l in range(w_in.shape[0])]

    def trunk(x):
        for p in params:
            x = _layer(x, p)
        return x

    return (trunk(x_prompt), trunk(x_sample))
```

```python
import functools

import numpy as np
import jax
import jax.numpy as jnp
from jax import lax
from jax.experimental import pallas as pl
from jax.experimental.pallas import tpu as pltpu

F32 = jnp.float32
BF16 = jnp.bfloat16

D_MODEL = 1024
HEAD_DIM = 64
ATTN_HEADS = 8
ATTN_WIDTH = ATTN_HEADS * HEAD_DIM
GLA_HEADS = 8
GLA_DK = 32
GLA_DV = 64
GLA_KW = GLA_HEADS * GLA_DK
GLA_VW = GLA_HEADS * GLA_DV
GATE_RANK = 16
GATE_TAU = 16.0
GLA_CHUNK = 32
DIL_PAIRS = ((128, 1), (512, 4), (2048, 16))
DILATIONS = tuple(d for _, d in DIL_PAIRS)
N_SIDE = 64
D_FF = 2816
EPS = 1e-6
MASKED_DIST = 1e30

LANES = 128
HEAD_PAIRS = ATTN_HEADS // 2
VMEM_LIMIT = 56 * 1024 * 1024

TM_PROJ = 1024
ATT_BQ = 128
ATT_BK = ATT_BQ + 2 * N_SIDE
SEG_PAD = N_SIDE
ATT_UNROLL = 32
GLA_T = 256
GLA_TILES = 4
FF_CHUNK = 256


def _cparams(*sem):
    return pltpu.CompilerParams(dimension_semantics=sem, vmem_limit_bytes=VMEM_LIMIT)


def _rms(x, g):
    ms = jnp.mean(x * x, axis=-1, keepdims=True)
    return x * lax.rsqrt(ms + EPS) * g


def _const_spec(shape):
    nd = len(shape)
    return pl.BlockSpec(shape, lambda *_: (0,) * nd, pipeline_mode=pl.Buffered(1))


def _inproj_kernel(x_ref, g_ref, wm_ref, wlr_ref, wg_ref, bg_ref,
                   q_ref, k_ref, v_ref, gq_ref, gk_ref, gv_ref, gr_ref, laf_ref, lab_ref):
    h = _rms(x_ref[...], g_ref[...]).astype(BF16)

    def proj(lo, hi):
        return jnp.dot(h, wm_ref[:, lo:hi], preferred_element_type=F32)

    lr = jnp.dot(h, wlr_ref[...], preferred_element_type=F32)
    a = ATTN_WIDTH
    q_ref[...] = (proj(0, a) * (HEAD_DIM ** -0.5)).astype(BF16)
    k_ref[...] = proj(a, 2 * a).astype(BF16)
    v_ref[...] = proj(2 * a, 3 * a).astype(BF16)

    pre = bg_ref[...] + jnp.dot(lr.astype(BF16), wg_ref[...], preferred_element_type=F32)
    la = (jnp.minimum(pre, 0.0) - jnp.log(1.0 + jnp.exp(-jnp.abs(pre)))) * (1.0 / GATE_TAU)
    laf_ref[...] = la[:, :GLA_KW]
    lab_ref[...] = la[:, GLA_KW:]

    o = 3 * a
    gq_ref[...] = proj(o, o + GLA_KW)
    gk_ref[...] = proj(o + GLA_KW, o + 2 * GLA_KW)
    o += 2 * GLA_KW
    gv_ref[...] = proj(o, o + GLA_VW).astype(BF16)
    gr_ref[...] = proj(o + GLA_VW, o + 2 * GLA_VW)


def _inproj(x2, g, wm, wlr, wg, bg):
    n = x2.shape[0]
    tm = TM_PROJ
    row = lambda w: pl.BlockSpec((tm, w), lambda i: (i, 0))
    outs = [(ATTN_WIDTH, BF16)] * 3 + [(GLA_KW, F32)] * 2 + [(GLA_VW, BF16), (GLA_VW, F32)] + [(GLA_KW, F32)] * 2
    return pl.pallas_call(
        _inproj_kernel,
        grid=(n // tm,),
        in_specs=[row(D_MODEL), _const_spec(g.shape), _const_spec(wm.shape), _const_spec(wlr.shape),
                  _const_spec(wg.shape), _const_spec(bg.shape)],
        out_specs=[row(w) for w, _ in outs],
        out_shape=[jax.ShapeDtypeStruct((n, w), dt) for w, dt in outs],
        compiler_params=_cparams("parallel"),
        name="inproj",
    )(x2, g, wm, wlr, wg, bg)


def _attn_kernel(slope_ref, dist_ref, q_ref, k_ref, v_ref, o_ref,
                 tok_ref, r4_ref, b1_ref, b4_ref, b16_ref, acc1_ref, acc4_ref, bias_ref, *, seq):
    s_len = seq
    l4, l16 = s_len // 4, s_len // 16
    p4, p16 = l4 + 2 * SEG_PAD, l16 + 2 * SEG_PAD
    srcs = (q_ref, k_ref, v_ref)
    lane = lax.broadcasted_iota(jnp.int32, (1, LANES), 1)
    first_head = lane < HEAD_DIM
    slope = slope_ref[0]

    zpad = jnp.zeros((SEG_PAD, LANES), BF16)

    def put_segment(buf_ref, a, base, rows, n):
        buf_ref[a, base:base + SEG_PAD, :] = zpad
        buf_ref[a, base + SEG_PAD:base + SEG_PAD + n, :] = rows
        buf_ref[a, base + SEG_PAD + n:base + 2 * SEG_PAD + n, :] = zpad

    for a in range(3):
        def upcast(i, carry, a=a):
            r0 = pl.multiple_of(i * 256, 256)
            tok_ref[a, pl.ds(r0, 256), :] = srcs[a][0, pl.ds(r0, 256), :].astype(F32)
            return carry

        lax.fori_loop(0, s_len // 256, upcast, 0)
        put_segment(b1_ref, a, 0, srcs[a][0], s_len)
        for r in range(4):
            t = tok_ref[a, pl.ds(r, l4, stride=4), :]
            r4_ref[a, r * l4:(r + 1) * l4, :] = t
            put_segment(b4_ref, a, r * p4, t.astype(BF16), l4)
        for r in range(4):
            for sub in range(4):
                t = r4_ref[a, pl.ds(r * l4 + sub, l16, stride=4), :]
                put_segment(b16_ref, a, (r + 4 * sub) * p16, t.astype(BF16), l16)

    for di, dil in enumerate(DILATIONS):
        for kind in range(dist_ref.shape[0]):
            for hh in range(2):
                c = slope[:, hh * HEAD_DIM:hh * HEAD_DIM + 1] * float(dil)
                bias_ref[di, kind, hh * ATT_BQ:(hh + 1) * ATT_BQ, :] = c * dist_ref[kind]

    def block(buf_ref, seg_base, m0, kind, di):
        krow = pl.multiple_of(seg_base + m0, N_SIDE)
        q = buf_ref[0, pl.ds(pl.multiple_of(krow + SEG_PAD, N_SIDE), ATT_BQ), :]
        k = buf_ref[1, pl.ds(krow, ATT_BK), :]
        v = buf_ref[2, pl.ds(krow, ATT_BK), :]
        zero = jnp.zeros_like(q)
        qs = jnp.concatenate([jnp.where(first_head, q, zero), jnp.where(first_head, zero, q)], axis=0)
        s = lax.dot_general(qs, k, (((1,), (1,)), ((), ())), preferred_element_type=F32) - bias_ref[di, kind]
        m = jnp.max(s, axis=-1, keepdims=True)
        p = jnp.exp(s - m).astype(BF16)
        pv = jnp.dot(p, jnp.concatenate([v, jnp.ones_like(v)], axis=1), preferred_element_type=F32)
        pick = lambda t: jnp.where(first_head, t[:ATT_BQ], t[ATT_BQ:])
        return pick(pv[:, :LANES]), pick(pv[:, LANES:]), pick(m)

    def kind_of(qb, nqb):
        return jnp.where(qb == 0, 1, 0) + jnp.where(qb == nqb - 1, 2, 0)

    def merge(a, b):
        m = jnp.maximum(a[2], b[2])
        ea, eb = jnp.exp(a[2] - m), jnp.exp(b[2] - m)
        return a[0] * ea + b[0] * eb, a[1] * ea + b[1] * eb, m

    def load(acc_ref, rows):
        return tuple(acc_ref[i, rows, :] for i in range(3))

    def store(acc_ref, rows, part):
        for i in range(3):
            acc_ref[i, rows, :] = part[i]

    n1, n4, n16 = s_len // ATT_BQ, l4 // ATT_BQ, l16 // ATT_BQ

    def branch1(qb, carry):
        m0 = pl.multiple_of(qb * ATT_BQ, ATT_BQ)
        store(acc1_ref, pl.ds(m0, ATT_BQ), block(b1_ref, 0, m0, kind_of(qb, n1), 0))
        return carry

    lax.fori_loop(0, n1, branch1, 0, unroll=ATT_UNROLL)

    def branch4(it, carry):
        r, qb = lax.div(it, n4), lax.rem(it, n4)
        m0 = pl.multiple_of(qb * ATT_BQ, ATT_BQ)
        row = pl.multiple_of(r * l4 + m0, ATT_BQ)
        store(acc4_ref, pl.ds(row, ATT_BQ), block(b4_ref, r * p4, m0, kind_of(qb, n4), 1))
        return carry

    lax.fori_loop(0, 4 * n4, branch4, 0, unroll=ATT_UNROLL)

    def branch16(it, carry):
        r, qb = lax.div(it, n16), lax.rem(it, n16)
        m0 = pl.multiple_of(qb * ATT_BQ, ATT_BQ)
        rows = pl.ds(lax.rem(r, 4) * l4 + lax.div(r, 4) + 4 * m0, ATT_BQ, stride=4)
        store(acc4_ref, rows, merge(load(acc4_ref, rows), block(b16_ref, r * p16, m0, kind_of(qb, n16), 2)))
        return carry

    lax.fori_loop(0, 16 * n16, branch16, 0, unroll=ATT_UNROLL)

    for r in range(4):
        def final(cb, carry, r=r):
            row = pl.multiple_of(r * l4 + cb * ATT_BQ, ATT_BQ)
            tok = pl.ds(r + 4 * ATT_BQ * cb, ATT_BQ, stride=4)
            num, den, _ = merge(load(acc1_ref, tok), load(acc4_ref, pl.ds(row, ATT_BQ)))
            o_ref[0, tok, :] = num / den
            return carry

        lax.fori_loop(0, n4, final, 0)


def _dist_table():
    i = np.arange(ATT_BQ)[:, None]
    j = np.arange(ATT_BK)[None, :]
    delta = j - N_SIDE - i
    ok = np.abs(delta) <= N_SIDE
    first = ok & ((delta >= 0) | (i >= N_SIDE))
    last = ok & ((delta < N_SIDE) | (i < N_SIDE))
    kinds = [ok, first, last, first & last]
    return jnp.asarray(np.stack([np.where(t, np.abs(delta), MASKED_DIST) for t in kinds]).astype(np.float32))


def _attention(q, k, v, slopes):
    b, s, _ = q.shape
    assert s % (16 * ATT_BQ) == 0
    blk = pl.BlockSpec((1, s, LANES), lambda bi, p: (bi, 0, p))
    dist = _dist_table()
    seg_rows = lambda d: d * (s // d + 2 * SEG_PAD)
    return pl.pallas_call(
        functools.partial(_attn_kernel, seq=s),
        grid=(b, HEAD_PAIRS),
        in_specs=[pl.BlockSpec((1, 1, LANES), lambda bi, p: (p, 0, 0)), _const_spec(dist.shape), blk, blk, blk],
        out_specs=blk,
        out_shape=jax.ShapeDtypeStruct((b, s, ATTN_WIDTH), F32),
        scratch_shapes=[pltpu.VMEM((3, s, LANES), F32), pltpu.VMEM((3, s, LANES), F32)]
        + [pltpu.VMEM((3, seg_rows(d), LANES), BF16) for d in (1, 4, 16)]
        + [pltpu.VMEM((3, s, LANES), F32)] * 2
        + [pltpu.VMEM((len(DILATIONS), dist.shape[0], 2 * ATT_BQ, ATT_BK), F32)],
        compiler_params=_cparams("parallel", "parallel"),
        name="attn",
    )(slopes, dist, q, k, v)


def _split2(x):
    hi = x.astype(BF16)
    return hi, (x - hi.astype(F32)).astype(BF16)


def _gla_kernel(*refs, reverse):
    state_ref = refs[-1]

    @pl.when(pl.program_id(1) == 0)
    def _():
        state_ref[...] = jnp.zeros_like(state_ref)

    def view(r, ti):
        return r.at[:, pl.ds(ti * GLA_T, GLA_T), :] if len(r.shape) == 3 else r

    order = range(GLA_TILES - 1, -1, -1) if reverse else range(GLA_TILES)
    tiles = [_gla_tile(*[view(r, ti) for r in refs[:-1]], state_ref, reverse=reverse) for ti in order]
    for _ in range(GLA_PHASES):
        for t in tiles:
            next(t, None)


GLA_PHASES = 3


def _gla_tile(*refs, reverse):
    if reverse:
        (triT_ref, q_ref, k_ref, v_ref, la_ref, of_ref, r_ref, gn_ref, o_ref, state_ref) = refs
    else:
        (triT_ref, q_ref, k_ref, v_ref, la_ref, o_ref, state_ref) = refs
    T, C, KW = GLA_T, GLA_CHUNK, GLA_KW
    nch, half, pairs = T // C, T // 2, GLA_HEADS // 2

    laT = la_ref[0].T
    bt = sum(jnp.dot(t, triT_ref[...], preferred_element_type=F32) for t in _split2(laT))
    bT, totT = bt[:, :T], bt[:, T:]
    kT = k_ref[0].T
    q_in = q_ref[0] * jnp.exp(bT.T) * (GLA_DK ** -0.5)
    q_bf = q_in.astype(BF16)
    k_intraT = kT * jnp.exp(-bT)
    k_stateT = (kT * jnp.exp(totT - bT)).astype(BF16)
    decayT = jnp.exp(totT)
    v = v_ref[0].astype(BF16)
    yield

    lane_k =lax.broadcasted_iota(jnp.int32, (1, KW), 1)
    lane_v = lax.broadcasted_iota(jnp.int32, (1, LANES), 1)
    first_half = lane_v < GLA_DV
    row_head = lax.broadcasted_iota(jnp.int32, (KW, 1), 0) // GLA_DK
    si = lax.broadcasted_iota(jnp.int32, (half, T), 0)
    sj = lax.broadcasted_iota(jnp.int32, (half, T), 1) % half
    causal = (si // C == sj // C) & ((sj >= si) if reverse else (sj <= si))

    intra = [[None] * 2 for _ in range(pairs)]
    for t2 in range(2):
        kk = k_intraT[:, t2 * half:(t2 + 1) * half]
        qq = q_bf[t2 * half:(t2 + 1) * half, :]
        for p in range(pairs):
            rhs = jnp.concatenate([jnp.where(row_head == 2 * p, kk, 0.0), jnp.where(row_head == 2 * p + 1, kk, 0.0)],
                                  axis=1).astype(BF16)
            s = jnp.dot(qq, rhs, preferred_element_type=F32)
            att = jnp.where(causal, s, 0.0).astype(BF16)
            vp = v[t2 * half:(t2 + 1) * half, p * LANES:(p + 1) * LANES]
            vbd = jnp.concatenate([jnp.where(first_half, vp, jnp.zeros_like(vp)),
                                   jnp.where(first_half, jnp.zeros_like(vp), vp)], axis=0)
            intra[p][t2] = jnp.dot(att, vbd, preferred_element_type=F32)

    par_row = lax.broadcasted_iota(jnp.int32, (2 * GLA_DK, LANES), 0) // GLA_DK
    par_lane = lax.broadcasted_iota(jnp.int32, (2 * GLA_DK, LANES), 1) // GLA_DV
    parity = par_row == par_lane
    u = [[None] * pairs for _ in range(nch)]
    for p in range(pairs):
        vp = v[:, p * LANES:(p + 1) * LANES]
        zero = jnp.zeros((C, LANES), BF16)
        vexp = jnp.concatenate(
            [jnp.concatenate([vp[m * C:(m + 1) * C] if m == n else zero for m in range(nch)], axis=0) for n in range(nch)],
            axis=1)
        up = jnp.dot(k_stateT[p * 2 * GLA_DK:(p + 1) * 2 * GLA_DK, :], vexp, preferred_element_type=F32)
        for n in range(nch):
            u[n][p] = jnp.where(parity, up[:, n * LANES:(n + 1) * LANES], 0.0)
    yield

    state = state_ref[...]
    seen = [None] * nch
    order = range(nch - 1, -1, -1) if reverse else range(nch)
    for n in order:
        seen[n] = state.astype(BF16)
        dcol = jnp.broadcast_to(decayT[:, n * C:n * C + 1], (KW, LANES))
        state = dcol * state + jnp.concatenate(u[n], axis=0)
    state_ref[...] = state

    inter = [[None] * nch for _ in range(GLA_HEADS // 2)]
    for n in range(nch):
        qn = q_in[n * C:(n + 1) * C, :]
        lhs = jnp.concatenate([jnp.where(lane_k // (2 * GLA_DK) == p, qn, 0.0) for p in range(GLA_HEADS // 2)],
                              axis=0).astype(BF16)
        res = jnp.dot(lhs, seen[n], preferred_element_type=F32)
        for p in range(GLA_HEADS // 2):
            inter[p][n] = res[p * C:(p + 1) * C, :]
    o = jnp.concatenate([jnp.concatenate(intra[p], axis=0) + jnp.concatenate(inter[p], axis=0) for p in range(pairs)],
                        axis=1)

    if not reverse:
        o_ref[0] = o
        return
    o = o + of_ref[0]
    r = r_ref[0]
    outs = []
    for p in range(GLA_HEADS // 2):
        x = o[:, p * LANES:(p + 1) * LANES]
        sq = x * x
        first = lane_v < GLA_DV
        s0 = jnp.sum(jnp.where(first, sq, 0.0), axis=-1, keepdims=True)
        s1 = jnp.sum(jnp.where(first, 0.0, sq), axis=-1, keepdims=True)
        ms = jnp.where(first, s0, s1) * (1.0 / GLA_DV)
        y = x * lax.rsqrt(ms + EPS) * gn_ref[...]
        rp = r[:, p * LANES:(p + 1) * LANES]
        outs.append(y * rp / (1.0 + jnp.exp(-rp)))
    o_ref[0] = jnp.concatenate(outs, axis=1).astype(o_ref.dtype)


def _gla_consts(reverse):
    i = np.arange(GLA_T)
    same = (i[:, None] // GLA_CHUNK) == (i[None, :] // GLA_CHUNK)
    tri = same & ((i[None, :] >= i[:, None]) if reverse else (i[None, :] <= i[:, None]))
    triT = np.concatenate([tri.T, same], axis=1)
    return jnp.asarray(triT, BF16)


def _gla_dir(gq, gk, gv, la, reverse, o_fwd=None, gr=None, gnorm=None):
    b, s, _ = gq.shape
    step = GLA_TILES * GLA_T
    assert s % step == 0
    nt = s // step
    tile = (lambda bi, t: (bi, nt - 1 - t, 0)) if reverse else (lambda bi, t: (bi, t, 0))
    kw = pl.BlockSpec((1, step, GLA_KW), tile)
    vw = pl.BlockSpec((1, step, GLA_VW), tile)
    triT = _gla_consts(reverse)
    args = [triT, gq, gk, gv, la]
    specs = [_const_spec(triT.shape), kw, kw, vw, kw]
    if reverse:
        args += [o_fwd, gr, gnorm]
        specs += [vw, vw, _const_spec(gnorm.shape)]
    return pl.pallas_call(
        functools.partial(_gla_kernel, reverse=reverse),
        grid=(b, nt),
        in_specs=specs,
        out_specs=vw,
        out_shape=jax.ShapeDtypeStruct((b, s, GLA_VW), BF16 if reverse else F32),
        scratch_shapes=[pltpu.VMEM((GLA_KW, LANES), F32)],
        compiler_params=_cparams("parallel", "arbitrary"),
        name="gla_bwd" if reverse else "gla_fwd",
    )(*args)


def _outproj_kernel(x_ref, oa_ref, og_ref, wa_ref, wg_ref, g_ref, y_ref):
    mix = jnp.dot(oa_ref[...].astype(BF16), wa_ref[...], preferred_element_type=F32)
    mix = mix + jnp.dot(og_ref[...], wg_ref[...], preferred_element_type=F32)
    y_ref[...] = x_ref[...] + _rms(mix, g_ref[...])


def _outproj(x2, oa, og, wa, wg, g):
    n = x2.shape[0]
    tm = TM_PROJ
    row = lambda w: pl.BlockSpec((tm, w), lambda i: (i, 0))
    return pl.pallas_call(
        _outproj_kernel,
        grid=(n // tm,),
        in_specs=[row(D_MODEL), row(ATTN_WIDTH), row(GLA_VW), _const_spec(wa.shape), _const_spec(wg.shape),
                  _const_spec(g.shape)],
        out_specs=row(D_MODEL),
        out_shape=jax.ShapeDtypeStruct((n, D_MODEL), F32),
        compiler_params=_cparams("parallel"),
        name="outproj",
    )(x2, oa, og, wa, wg, g)


def _gelu_tanh(y):
    return 0.5 * y * (1.0 + jnp.tanh(0.7978845608028654 * (y + 0.044715 * (y * y * y))))


def _ffn_kernel(x_ref, xp_ref, xn_ref, g_ref, wup_ref, cw_ref, cb_ref, wdn_ref, gp_ref, y_ref, act_ref, *, tiles_per_seq):
    tm = x_ref.shape[0]
    i = pl.program_id(0)
    first = (i % tiles_per_seq) == 0
    last = (i % tiles_per_seq) == tiles_per_seq - 1
    x = x_ref[...]
    g = g_ref[...]
    h = _rms(x, g).astype(BF16)
    halo = _rms(jnp.concatenate([xp_ref[...], xn_ref[...]], axis=0), g).astype(BF16)
    row = lax.broadcasted_iota(jnp.int32, (tm, 1), 0)
    for lo in range(0, D_FF, FF_CHUNK):
        hi = min(lo + FF_CHUNK, D_FF)
        wa = wup_ref[:, lo:hi]
        a = jnp.dot(h, wa, preferred_element_type=F32)
        gate = jnp.dot(h, wup_ref[:, D_FF + lo:D_FF + hi], preferred_element_type=F32)
        ah = jnp.dot(halo, wa, preferred_element_type=F32)
        a_prev = jnp.where(first, 0.0, ah[7:8, :])
        a_next = jnp.where(last, 0.0, ah[8:9, :])
        a_dn = jnp.where(row == 0, a_prev, pltpu.roll(a, 1, axis=0))
        a_up = jnp.where(row == tm - 1, a_next, pltpu.roll(a, tm - 1, axis=0))
        cw = cw_ref[:, lo:hi]
        y = cb_ref[:, lo:hi] + a_dn * cw[0:1, :] + a * cw[1:2, :] + a_up * cw[2:3, :]
        act_ref[:, lo:hi] = (_gelu_tanh(y) * gate).astype(BF16)
    f = jnp.dot(act_ref[...], wdn_ref[...], preferred_element_type=F32)
    y_ref[...] = x + _rms(f, gp_ref[...])


def _ffn(x2, seq, g, wup, cw, cb, wdn, gp):
    n = x2.shape[0]
    tm = TM_PROJ
    assert seq % tm == 0
    per, nblk = tm // 8, n // 8
    consts = (g, wup, cw, cb, wdn, gp)
    return pl.pallas_call(
        functools.partial(_ffn_kernel, tiles_per_seq=seq // tm),
        grid=(n // tm,),
        in_specs=[pl.BlockSpec((tm, D_MODEL), lambda i: (i, 0)),
                  pl.BlockSpec((8, D_MODEL), lambda i: (jnp.maximum(i * per - 1, 0), 0)),
                  pl.BlockSpec((8, D_MODEL), lambda i: (jnp.minimum((i + 1) * per, nblk - 1), 0))]
        + [_const_spec(c.shape) for c in consts],
        out_specs=pl.BlockSpec((tm, D_MODEL), lambda i: (i, 0)),
        out_shape=jax.ShapeDtypeStruct((n, D_MODEL), F32),
        scratch_shapes=[pltpu.VMEM((tm, D_FF), BF16)],
        compiler_params=_cparams("parallel"),
        name="ffn",
    )(x2, x2, x2, *consts)


def _layer_params(l,norm_mix_pre, w_in, w_gate_fwd, b_gate_fwd, w_gate_bwd, b_gate_bwd, gla_norm, w_out,
                  norm_mix_post, norm_ffn_pre, w_up, conv_w, conv_b, w_down, norm_ffn_post):
    n_main = 3 * ATTN_WIDTH + 2 * GLA_KW + 2 * GLA_VW
    wl = w_in[l]
    row = lambda t: t.reshape(1, -1).astype(F32)
    wlr = jnp.pad(wl[:, n_main:], ((0, 0), (0, LANES - GATE_RANK))).astype(BF16)
    wg = jnp.pad(jnp.concatenate([w_gate_fwd[l], w_gate_bwd[l]], axis=1), ((0, LANES - GATE_RANK), (0, 0)))
    bg = jnp.concatenate([b_gate_fwd[l], b_gate_bwd[l]]).reshape(1, -1)
    slopes = jnp.exp2(-8.0 * jnp.arange(1, ATTN_HEADS + 1, dtype=F32) / ATTN_HEADS)
    return dict(
        n_pre=row(norm_mix_pre[l]), wm=wl[:, :n_main].astype(BF16), wlr=wlr, wg=wg.astype(BF16), bg=bg.astype(F32),
        slopes=jnp.repeat(slopes, HEAD_DIM).reshape(HEAD_PAIRS, 1, LANES),
        gnorm=jnp.tile(gla_norm[l].astype(F32), 2).reshape(1, LANES),
        wo_a=w_out[l][:ATTN_WIDTH].astype(BF16), wo_g=w_out[l][ATTN_WIDTH:].astype(BF16), n_post=row(norm_mix_post[l]),
        n_ffn_pre=row(norm_ffn_pre[l]), wup=w_up[l].astype(BF16), cw=conv_w[l].astype(F32), cb=row(conv_b[l]),
        wdn=w_down[l].astype(BF16), n_ffn_post=row(norm_ffn_post[l]),
    )


def _layer(x, p):
    b, s, _ = x.shape
    x2 = x.reshape(b * s, D_MODEL)
    q, k, v, gq, gk, gv, gr, laf, lab = _inproj(x2, p["n_pre"], p["wm"], p["wlr"], p["wg"], p["bg"])
    seq3 = lambda t: t.reshape(b, s, t.shape[-1])
    o_attn = _attention(seq3(q), seq3(k), seq3(v), p["slopes"])
    o_fwd = _gla_dir(seq3(gq), seq3(gk), seq3(gv), seq3(laf), reverse=False)
    o_gla = _gla_dir(seq3(gq), seq3(gk), seq3(gv), seq3(lab), reverse=True, o_fwd=o_fwd, gr=seq3(gr), gnorm=p["gnorm"])
    x2 = _outproj(x2, o_attn.reshape(b * s, ATTN_WIDTH), o_gla.reshape(b * s, GLA_VW), p["wo_a"], p["wo_g"],
                  p["n_post"])
    x2 = _ffn(x2, s, p["n_ffn_pre"], p["wup"], p["cw"], p["cb"], p["wdn"], p["n_ffn_post"])
    return x2.reshape(b, s, D_MODEL)


def kernel(x_prompt, x_sample, norm_mix_pre, w_in, w_gate_fwd, b_gate_fwd, w_gate_bwd, b_gate_bwd, gla_norm, w_out,
           norm_mix_post, norm_ffn_pre, w_up, conv_w, conv_b, w_down, norm_ffn_post):
    weights = (norm_mix_pre, w_in, w_gate_fwd, b_gate_fwd, w_gate_bwd, b_gate_bwd, gla_norm, w_out, norm_mix_post,
               norm_ffn_pre, w_up, conv_w, conv_b, w_down, norm_ffn_post)
    params = [_layer_params(l, *weights) for l in range(w_in.shape[0])]

    def trunk(x):
        for p in params:
            x = _layer(x, p)
        return x

    return (trunk(x_prompt), trunk(x_sample))
```

```python
import functools

import numpy as np
import jax
import jax.numpy as jnp
from jax import lax
from jax.experimental import pallas as pl
from jax.experimental.pallas import tpu as pltpu

F32 = jnp.float32
BF16 = jnp.bfloat16

D_MODEL = 1024
HEAD_DIM = 64
ATTN_HEADS = 8
ATTN_WIDTH = ATTN_HEADS * HEAD_DIM
GLA_HEADS = 8
GLA_DK = 32
GLA_DV = 64
GLA_KW = GLA_HEADS * GLA_DK
GLA_VW = GLA_HEADS * GLA_DV
GATE_RANK = 16
GATE_TAU = 16.0
GLA_CHUNK = 32
DIL_PAIRS = ((128, 1), (512, 4), (2048, 16))
DILATIONS = tuple(d for _, d in DIL_PAIRS)
N_SIDE = 64
D_FF = 2816
EPS = 1e-6
MASKED_DIST = 1e30

LANES = 128
HEAD_PAIRS = ATTN_HEADS // 2
VMEM_LIMIT = 56 * 1024 * 1024

TM_PROJ = 1024
ATT_BQ = 128
ATT_BK = ATT_BQ + 2 * N_SIDE
SEG_PAD = N_SIDE
ATT_UNROLL = 32
GLA_T = 256
GLA_TILES = 4
FF_CHUNK = 256


def _cparams(*sem):
    return pltpu.CompilerParams(dimension_semantics=sem, vmem_limit_bytes=VMEM_LIMIT)


def _rms(x, g):
    ms = jnp.mean(x * x, axis=-1, keepdims=True)
    return x * lax.rsqrt(ms + EPS) * g


def _const_spec(shape):
    nd = len(shape)
    return pl.BlockSpec(shape, lambda *_: (0,) * nd, pipeline_mode=pl.Buffered(1))


def _inproj_kernel(x_ref, g_ref, wm_ref, wlr_ref, wg_ref, bg_ref,
                   q_ref, k_ref, v_ref, gq_ref, gk_ref, gv_ref, gr_ref, laf_ref, lab_ref):
    h = _rms(x_ref[...], g_ref[...]).astype(BF16)

    def proj(lo, hi):
        return jnp.dot(h, wm_ref[:, lo:hi], preferred_element_type=F32)

    lr = jnp.dot(h, wlr_ref[...], preferred_element_type=F32)
    a = ATTN_WIDTH
    q_ref[...] = (proj(0, a) * (HEAD_DIM ** -0.5)).astype(BF16)
    k_ref[...] = proj(a, 2 * a).astype(BF16)
    v_ref[...] = proj(2 * a, 3 * a).astype(BF16)

    pre = bg_ref[...] + jnp.dot(lr.astype(BF16), wg_ref[...], preferred_element_type=F32)
    la = (jnp.minimum(pre, 0.0) - jnp.log(1.0 + jnp.exp(-jnp.abs(pre)))) * (1.0 / GATE_TAU)
    laf_ref[...] = la[:, :GLA_KW]
    lab_ref[...] = la[:, GLA_KW:]

    o = 3 * a
    gq_ref[...] = proj(o, o + GLA_KW)
    gk_ref[...] = proj(o + GLA_KW, o + 2 * GLA_KW)
    o += 2 * GLA_KW
    gv_ref[...] = proj(o, o + GLA_VW).astype(BF16)
    gr_ref[...] = proj(o + GLA_VW, o + 2 * GLA_VW)


def _inproj(x2, g, wm, wlr, wg, bg):
    n = x2.shape[0]
    tm = TM_PROJ
    row = lambda w: pl.BlockSpec((tm, w), lambda i: (i, 0))
    outs = [(ATTN_WIDTH, BF16)] * 3 + [(GLA_KW, F32)] * 2 + [(GLA_VW, BF16), (GLA_VW, F32)] + [(GLA_KW, F32)] * 2
    return pl.pallas_call(
        _inproj_kernel,
        grid=(n // tm,),
        in_specs=[row(D_MODEL), _const_spec(g.shape), _const_spec(wm.shape), _const_spec(wlr.shape),
                  _const_spec(wg.shape), _const_spec(bg.shape)],
        out_specs=[row(w) for w, _ in outs],
        out_shape=[jax.ShapeDtypeStruct((n, w), dt) for w, dt in outs],
        compiler_params=_cparams("parallel"),
        name="inproj",
    )(x2, g, wm, wlr, wg, bg)


def _attn_kernel(slope_ref, dist_ref, q_ref, k_ref, v_ref, o_ref,
                 tok_ref, r4_ref, b1_ref, b4_ref, b16_ref, acc1_ref, acc4_ref, bias_ref, *, seq):
    s_len = seq
    l4, l16 = s_len // 4, s_len // 16
    p4, p16 = l4 + 2 * SEG_PAD, l16 + 2 * SEG_PAD
    srcs = (q_ref, k_ref, v_ref)
    lane = lax.broadcasted_iota(jnp.int32, (1, LANES), 1)
    first_head = lane < HEAD_DIM
    slope = slope_ref[0]

    @pl.when(pl.program_id(1) == 0)
    def _():
        zpad = jnp.zeros((SEG_PAD, LANES), BF16)
        for buf_ref, n_seg, n in ((b1_ref, 1, s_len), (b4_ref, 4, l4), (b16_ref, 16, l16)):
            for a in range(3):
                for seg in range(n_seg):
                    base = seg * (n + 2 * SEG_PAD)
                    buf_ref[a, base:base + SEG_PAD, :] = zpad
                    buf_ref[a, base + SEG_PAD + n:base + 2 * SEG_PAD + n, :] = zpad
        for di, dil in enumerate(DILATIONS):
            for kind in range(dist_ref.shape[0]):
                for hh in range(2):
                    c = slope[:, hh * HEAD_DIM:hh * HEAD_DIM + 1] * float(dil)
                    bias_ref[di, kind, hh * ATT_BQ:(hh + 1) * ATT_BQ, :] = c * dist_ref[kind]

    def put_segment(buf_ref, a, base, rows, n):
        buf_ref[a, base + SEG_PAD:base + SEG_PAD + n, :] = rows

    for a in range(3):
        def upcast(i, carry, a=a):
            r0 = pl.multiple_of(i * 256, 256)
            tok_ref[a, pl.ds(r0, 256), :] = srcs[a][0, pl.ds(r0, 256), :].astype(F32)
            return carry

        lax.fori_loop(0, s_len // 256, upcast, 0)
        put_segment(b1_ref, a, 0, srcs[a][0], s_len)
        for r in range(4):
            t = tok_ref[a, pl.ds(r, l4, stride=4), :]
            r4_ref[a, r * l4:(r + 1) * l4, :] = t
            put_segment(b4_ref, a, r * p4, t.astype(BF16), l4)
        for r in range(4):
            for sub in range(4):
                t = r4_ref[a, pl.ds(r * l4 + sub, l16, stride=4), :]
                put_segment(b16_ref, a, (r + 4 * sub) * p16, t.astype(BF16), l16)

    def block(buf_ref, seg_base, m0, kind, di):
        krow = pl.multiple_of(seg_base + m0, N_SIDE)
        q = buf_ref[0, pl.ds(pl.multiple_of(krow + SEG_PAD, N_SIDE), ATT_BQ), :]
        k = buf_ref[1, pl.ds(krow, ATT_BK), :]
        v = buf_ref[2, pl.ds(krow, ATT_BK), :]
        zero = jnp.zeros_like(q)
        qs = jnp.concatenate([jnp.where(first_head, q, zero), jnp.where(first_head, zero, q)], axis=0)
        s = lax.dot_general(qs, k, (((1,), (1,)), ((), ())), preferred_element_type=F32) - bias_ref[di, kind]
        m = jnp.max(s, axis=-1, keepdims=True)
        p = jnp.exp(s - m).astype(BF16)
        pv = jnp.dot(p, jnp.concatenate([v, jnp.ones_like(v)], axis=1), preferred_element_type=F32)
        pick = lambda t: jnp.where(first_head, t[:ATT_BQ], t[ATT_BQ:])
        return pick(pv[:, :LANES]), pick(pv[:, LANES:]), pick(m)

    def kind_of(qb, nqb):
        return jnp.where(qb == 0, 1, 0) + jnp.where(qb == nqb - 1, 2, 0)

    def merge(a, b):
        m = jnp.maximum(a[2], b[2])
        ea, eb = jnp.exp(a[2] - m), jnp.exp(b[2] - m)
        return a[0] * ea + b[0] * eb, a[1] * ea + b[1] * eb, m

    def load(acc_ref, rows):
        return tuple(acc_ref[i, rows, :] for i in range(3))

    def store(acc_ref, rows, part):
        for i in range(3):
            acc_ref[i, rows, :] = part[i]

    n1, n4, n16 = s_len // ATT_BQ, l4 // ATT_BQ, l16 // ATT_BQ

    def branch1(qb, carry):
        m0 = pl.multiple_of(qb * ATT_BQ, ATT_BQ)
        store(acc1_ref, pl.ds(m0, ATT_BQ), block(b1_ref, 0, m0, kind_of(qb, n1), 0))
        return carry

    lax.fori_loop(0, n1, branch1, 0, unroll=ATT_UNROLL)

    def branch4(it, carry):
        r, qb = lax.div(it, n4), lax.rem(it, n4)
        m0 = pl.multiple_of(qb * ATT_BQ, ATT_BQ)
        row = pl.multiple_of(r * l4 + m0, ATT_BQ)
        store(acc4_ref, pl.ds(row, ATT_BQ), block(b4_ref, r * p4, m0, kind_of(qb, n4), 1))
        return carry

    lax.fori_loop(0, 4 * n4, branch4, 0, unroll=ATT_UNROLL)

    def branch16(it, carry):
        r, qb = lax.div(it, n16), lax.rem(it, n16)
        m0 = pl.multiple_of(qb * ATT_BQ, ATT_BQ)
        rows = pl.ds(lax.rem(r, 4) * l4 + lax.div(r, 4) + 4 * m0, ATT_BQ, stride=4)
        store(acc4_ref, rows, merge(load(acc4_ref, rows), block(b16_ref, r * p16, m0, kind_of(qb, n16), 2)))
        return carry

    lax.fori_loop(0, 16 * n16, branch16, 0, unroll=ATT_UNROLL)

    for r in range(4):
        def final(cb, carry, r=r):
            row = pl.multiple_of(r * l4 + cb * ATT_BQ, ATT_BQ)
            tok = pl.ds(r + 4 * ATT_BQ * cb, ATT_BQ, stride=4)
            num, den, _ = merge(load(acc1_ref, tok), load(acc4_ref, pl.ds(row, ATT_BQ)))
            o_ref[0, tok, :] = num / den
            return carry

        lax.fori_loop(0, n4, final, 0, unroll=True)


def _dist_table():
    i = np.arange(ATT_BQ)[:, None]
    j = np.arange(ATT_BK)[None, :]
    delta = j - N_SIDE - i
    ok = np.abs(delta) <= N_SIDE
    first = ok & ((delta >= 0) | (i >= N_SIDE))
    last = ok & ((delta < N_SIDE) | (i < N_SIDE))
    kinds = [ok, first, last, first & last]
    return jnp.asarray(np.stack([np.where(t, np.abs(delta), MASKED_DIST) for t in kinds]).astype(np.float32))


def _attention(q, k, v, slopes):
    b, s, _ = q.shape
    assert s % (16 * ATT_BQ) == 0
    blk = pl.BlockSpec((1, s, LANES), lambda p, bi: (bi, 0, p))
    dist = _dist_table()
    seg_rows = lambda d: d * (s // d + 2 * SEG_PAD)
    return pl.pallas_call(
        functools.partial(_attn_kernel, seq=s),
        grid=(HEAD_PAIRS, b),
        in_specs=[pl.BlockSpec((1, 1, LANES), lambda p, bi: (p, 0, 0)), _const_spec(dist.shape), blk, blk, blk],
        out_specs=blk,
        out_shape=jax.ShapeDtypeStruct((b, s, ATTN_WIDTH), F32),
        scratch_shapes=[pltpu.VMEM((3, s, LANES), F32), pltpu.VMEM((3, s, LANES), F32)]
        + [pltpu.VMEM((3, seg_rows(d), LANES), BF16) for d in (1, 4, 16)]
        + [pltpu.VMEM((3, s, LANES), F32)] * 2
        + [pltpu.VMEM((len(DILATIONS), dist.shape[0], 2 * ATT_BQ, ATT_BK), F32)],
        compiler_params=_cparams("arbitrary", "arbitrary"),
        name="attn",
    )(slopes, dist, q, k, v)


def _split2(x):
    hi = x.astype(BF16)
    return hi, (x - hi.astype(F32)).astype(BF16)


def _gla_kernel(*refs, reverse):
    state_ref = refs[-1]

    @pl.when(pl.program_id(1) == 0)
    def _():
        state_ref[...] = jnp.zeros_like(state_ref)

    def view(r, ti):
        return r.at[:, pl.ds(ti * GLA_T, GLA_T), :] if len(r.shape) == 3 else r

    order = range(GLA_TILES - 1, -1, -1) if reverse else range(GLA_TILES)
    tiles = [_gla_tile(*[view(r, ti) for r in refs[:-1]], state_ref, reverse=reverse) for ti in order]
    for _ in range(GLA_PHASES):
        for t in tiles:
            next(t, None)


GLA_PHASES = 3


def _gla_tile(*refs, reverse):
    if reverse:
        (triT_ref, q_ref, k_ref, v_ref, la_ref, of_ref, r_ref, gn_ref, o_ref, state_ref) = refs
    else:
        (triT_ref, q_ref, k_ref, v_ref, la_ref, o_ref, state_ref) = refs
    T, C, KW = GLA_T, GLA_CHUNK, GLA_KW
    nch, half, pairs = T // C, T // 2, GLA_HEADS // 2

    laT = la_ref[0].T
    bt = sum(jnp.dot(t, triT_ref[...], preferred_element_type=F32) for t in _split2(laT))
    bT, totT = bt[:, :T], bt[:, T:]
    kT = k_ref[0].T
    q_in = q_ref[0] * jnp.exp(bT.T) * (GLA_DK ** -0.5)
    q_bf = q_in.astype(BF16)
    k_intraT = kT * jnp.exp(-bT)
    k_stateT = (kT * jnp.exp(totT - bT)).astype(BF16)
    decayT = jnp.exp(totT)
    v = v_ref[0].astype(BF16)
    yield

    lane_k = lax.broadcasted_iota(jnp.int32, (1, KW), 1)
    lane_v = lax.broadcasted_iota(jnp.int32, (1, LANES), 1)
    first_half = lane_v < GLA_DV
    row_head = lax.broadcasted_iota(jnp.int32, (KW, 1), 0) // GLA_DK
    si = lax.broadcasted_iota(jnp.int32, (half, T), 0)
    sj = lax.broadcasted_iota(jnp.int32, (half, T), 1) % half
    causal = (si // C == sj // C) & ((sj >= si) if reverse else (sj <= si))

    intra = [[None] * 2 for _ in range(pairs)]
    for t2 in range(2):
        kk = k_intraT[:, t2 * half:(t2 + 1) * half]
        qq = q_bf[t2 * half:(t2 + 1) * half, :]
        for p in range(pairs):
            rhs = jnp.concatenate([jnp.where(row_head == 2 * p, kk, 0.0), jnp.where(row_head == 2 * p + 1, kk, 0.0)],
                                  axis=1).astype(BF16)
            s = jnp.dot(qq, rhs, preferred_element_type=F32)
            att = jnp.where(causal, s, 0.0).astype(BF16)
            vp = v[t2 * half:(t2 + 1) * half, p * LANES:(p + 1) * LANES]
            vbd = jnp.concatenate([jnp.where(first_half, vp, jnp.zeros_like(vp)),
                                   jnp.where(first_half, jnp.zeros_like(vp), vp)], axis=0)
            intra[p][t2] = jnp.dot(att, vbd, preferred_element_type=F32)

    par_row = lax.broadcasted_iota(jnp.int32, (2 * GLA_DK, LANES), 0) // GLA_DK
    par_lane = lax.broadcasted_iota(jnp.int32, (2 * GLA_DK, LANES), 1) // GLA_DV
    parity = par_row == par_lane
    u = [[None] * pairs for _ in range(nch)]
    for p in range(pairs):
        vp = v[:, p * LANES:(p + 1) * LANES]
        zero = jnp.zeros((C, LANES), BF16)
        vexp = jnp.concatenate(
            [jnp.concatenate([vp[m * C:(m + 1) * C] if m == n else zero for m in range(nch)], axis=0) for n in range(nch)],
            axis=1)
        up = jnp.dot(k_stateT[p * 2 * GLA_DK:(p + 1) * 2 * GLA_DK, :], vexp, preferred_element_type=F32)
        for n in range(nch):
            u[n][p] = jnp.where(parity, up[:, n * LANES:(n + 1) * LANES], 0.0)
    yield

    state = state_ref[...]
    seen = [None] * nch
    order = range(nch - 1, -1, -1) if reverse else range(nch)
    for n in order:
        seen[n] = state.astype(BF16)
        dcol = jnp.broadcast_to(decayT[:, n * C:n * C + 1], (KW, LANES))
        state = dcol * state + jnp.concatenate(u[n], axis=0)
    state_ref[...] = state

    inter = [[None] * nch for _ in range(GLA_HEADS // 2)]
    for n in range(nch):
        qn = q_in[n * C:(n + 1) * C, :]
        lhs = jnp.concatenate([jnp.where(lane_k // (2 * GLA_DK) == p, qn, 0.0) for p in range(GLA_HEADS // 2)],
                              axis=0).astype(BF16)
        res = jnp.dot(lhs, seen[n], preferred_element_type=F32)
        for p in range(GLA_HEADS // 2):
            inter[p][n] = res[p * C:(p + 1) * C, :]
    o = jnp.concatenate([jnp.concatenate(intra[p], axis=0) + jnp.concatenate(inter[p], axis=0) for p in range(pairs)],
                        axis=1)

    if not reverse:
        o_ref[0] = o
        return
    o = o + of_ref[0]
    r = r_ref[0]
    outs = []
    for p in range(GLA_HEADS // 2):
        x = o[:, p * LANES:(p + 1) * LANES]
        sq = x * x
        first = lane_v < GLA_DV
        s0 = jnp.sum(jnp.where(first, sq, 0.0), axis=-1, keepdims=True)
        s1 = jnp.sum(jnp.where(first, 0.0, sq), axis=-1, keepdims=True)
        ms = jnp.where(first, s0, s1) * (1.0 / GLA_DV)
        y = x * lax.rsqrt(ms + EPS) * gn_ref[...]
        rp = r[:, p * LANES:(p + 1) * LANES]
        outs.append(y * rp / (1.0 + jnp.exp(-rp)))
    o_ref[0] = jnp.concatenate(outs, axis=1).astype(o_ref.dtype)


def _gla_consts(reverse):
    i = np.arange(GLA_T)
    same = (i[:, None] // GLA_CHUNK) == (i[None, :] // GLA_CHUNK)
    tri = same & ((i[None, :] >= i[:, None]) if reverse else (i[None, :] <= i[:, None]))
    triT = np.concatenate([tri.T, same], axis=1)
    return jnp.asarray(triT, BF16)


def _gla_dir(gq, gk, gv, la, reverse, o_fwd=None, gr=None, gnorm=None):
    b, s, _ = gq.shape
    step = GLA_TILES * GLA_T
    assert s % step == 0
    nt = s // step
    tile = (lambda bi, t: (bi, nt - 1 - t, 0)) if reverse else (lambda bi, t: (bi, t, 0))
    kw = pl.BlockSpec((1, step, GLA_KW), tile)
    vw = pl.BlockSpec((1, step, GLA_VW), tile)
    triT = _gla_consts(reverse)
    args = [triT, gq, gk, gv, la]
    specs = [_const_spec(triT.shape), kw, kw, vw, kw]
    if reverse:
        args += [o_fwd, gr, gnorm]
        specs += [vw, vw, _const_spec(gnorm.shape)]
    return pl.pallas_call(
        functools.partial(_gla_kernel, reverse=reverse),
        grid=(b, nt),
        in_specs=specs,
        out_specs=vw,
        out_shape=jax.ShapeDtypeStruct((b, s, GLA_VW), BF16 if reverse else F32),
        scratch_shapes=[pltpu.VMEM((GLA_KW, LANES), F32)],
        compiler_params=_cparams("parallel", "arbitrary"),
        name="gla_bwd" if reverse else "gla_fwd",
    )(*args)


def _outproj_kernel(x_ref, oa_ref, og_ref, wa_ref, wg_ref, g_ref, y_ref):
    mix = jnp.dot(oa_ref[...].astype(BF16), wa_ref[...], preferred_element_type=F32)
    mix = mix + jnp.dot(og_ref[...], wg_ref[...], preferred_element_type=F32)
    y_ref[...] = x_ref[...] + _rms(mix, g_ref[...])


def _outproj(x2, oa, og, wa, wg, g):
    n = x2.shape[0]
    tm = TM_PROJ
    row = lambda w: pl.BlockSpec((tm, w), lambda i: (i, 0))
    return pl.pallas_call(
        _outproj_kernel,
        grid=(n // tm,),
        in_specs=[row(D_MODEL), row(ATTN_WIDTH), row(GLA_VW), _const_spec(wa.shape), _const_spec(wg.shape),
                  _const_spec(g.shape)],
        out_specs=row(D_MODEL),
        out_shape=jax.ShapeDtypeStruct((n, D_MODEL), F32),
        compiler_params=_cparams("parallel"),
        name="outproj",
    )(x2, oa, og, wa, wg, g)


def _gelu_tanh(y):
    return 0.5 * y * (1.0 + jnp.tanh(0.7978845608028654 * (y + 0.044715 * (y * y * y))))


def _ffn_kernel(x_ref, xp_ref, xn_ref, g_ref, wup_ref, cw_ref, cb_ref, wdn_ref, gp_ref, y_ref, act_ref, *, tiles_per_seq):
    tm = x_ref.shape[0]
    i = pl.program_id(0)
    first = (i % tiles_per_seq) == 0
    last = (i % tiles_per_seq) == tiles_per_seq - 1
    x = x_ref[...]
    g = g_ref[...]
    h = _rms(x, g).astype(BF16)
    halo = _rms(jnp.concatenate([xp_ref[...], xn_ref[...]], axis=0), g).astype(BF16)
    row = lax.broadcasted_iota(jnp.int32, (tm, 1), 0)
    for lo in range(0, D_FF, FF_CHUNK):
        hi = min(lo + FF_CHUNK, D_FF)
        wa = wup_ref[:, lo:hi]
        a = jnp.dot(h, wa, preferred_element_type=F32)
        gate = jnp.dot(h, wup_ref[:, D_FF + lo:D_FF + hi], preferred_element_type=F32)
        ah = jnp.dot(halo, wa, preferred_element_type=F32)
        a_prev = jnp.where(first, 0.0, ah[7:8, :])
        a_next = jnp.where(last, 0.0, ah[8:9, :])
        a_dn = jnp.where(row == 0, a_prev, pltpu.roll(a, 1, axis=0))
        a_up = jnp.where(row == tm - 1, a_next, pltpu.roll(a, tm - 1, axis=0))
        cw = cw_ref[:, lo:hi]
        y = cb_ref[:, lo:hi] + a_dn * cw[0:1, :] + a * cw[1:2, :] + a_up * cw[2:3, :]
        act_ref[:, lo:hi] = (_gelu_tanh(y) * gate).astype(BF16)
    f = jnp.dot(act_ref[...], wdn_ref[...], preferred_element_type=F32)
    y_ref[...] = x + _rms(f, gp_ref[...])


def _ffn(x2, seq, g, wup, cw, cb, wdn, gp):
    n = x2.shape[0]
    tm = TM_PROJ
    assert seq % tm == 0
    per, nblk = tm // 8, n // 8
    consts = (g, wup, cw, cb, wdn, gp)
    return pl.pallas_call(
        functools.partial(_ffn_kernel, tiles_per_seq=seq // tm),
        grid=(n // tm,),
        in_specs=[pl.BlockSpec((tm, D_MODEL), lambda i: (i, 0)),
                  pl.BlockSpec((8, D_MODEL), lambda i: (jnp.maximum(i * per - 1, 0), 0)),
                  pl.BlockSpec((8, D_MODEL), lambda i: (jnp.minimum((i + 1) * per, nblk - 1), 0))]
        + [_const_spec(c.shape) for c in consts],
        out_specs=pl.BlockSpec((tm, D_MODEL), lambda i: (i, 0)),
        out_shape=jax.ShapeDtypeStruct((n, D_MODEL), F32),
        scratch_shapes=[pltpu.VMEM((tm, D_FF), BF16)],
        compiler_params=_cparams("parallel"),
        name="ffn",
    )(x2, x2, x2, *consts)


def _layer_params(l,norm_mix_pre, w_in, w_gate_fwd, b_gate_fwd, w_gate_bwd, b_gate_bwd, gla_norm, w_out,
                  norm_mix_post, norm_ffn_pre, w_up, conv_w, conv_b, w_down, norm_ffn_post):
    n_main = 3 * ATTN_WIDTH + 2 * GLA_KW + 2 * GLA_VW
    wl = w_in[l]
    row = lambda t: t.reshape(1, -1).astype(F32)
    wlr = jnp.pad(wl[:, n_main:], ((0, 0), (0, LANES - GATE_RANK))).astype(BF16)
    wg = jnp.pad(jnp.concatenate([w_gate_fwd[l], w_gate_bwd[l]], axis=1), ((0, LANES - GATE_RANK), (0, 0)))
    bg = jnp.concatenate([b_gate_fwd[l], b_gate_bwd[l]]).reshape(1, -1)
    slopes = jnp.exp2(-8.0 * jnp.arange(1, ATTN_HEADS + 1, dtype=F32) / ATTN_HEADS)
    return dict(
        n_pre=row(norm_mix_pre[l]), wm=wl[:, :n_main].astype(BF16), wlr=wlr, wg=wg.astype(BF16), bg=bg.astype(F32),
        slopes=jnp.repeat(slopes, HEAD_DIM).reshape(HEAD_PAIRS, 1, LANES),
        gnorm=jnp.tile(gla_norm[l].astype(F32), 2).reshape(1, LANES),
        wo_a=w_out[l][:ATTN_WIDTH].astype(BF16), wo_g=w_out[l][ATTN_WIDTH:].astype(BF16), n_post=row(norm_mix_post[l]),
        n_ffn_pre=row(norm_ffn_pre[l]), wup=w_up[l].astype(BF16), cw=conv_w[l].astype(F32), cb=row(conv_b[l]),
        wdn=w_down[l].astype(BF16), n_ffn_post=row(norm_ffn_post[l]),
    )


def _layer(x, p):
    b, s, _ = x.shape
    x2 = x.reshape(b * s, D_MODEL)
    q, k, v, gq, gk, gv, gr, laf, lab = _inproj(x2, p["n_pre"], p["wm"], p["wlr"], p["wg"], p["bg"])
    seq3 = lambda t: t.reshape(b, s, t.shape[-1])
    o_attn = _attention(seq3(q), seq3(k), seq3(v), p["slopes"])
    o_fwd = _gla_dir(seq3(gq), seq3(gk), seq3(gv), seq3(laf), reverse=False)
    o_gla = _gla_dir(seq3(gq), seq3(gk), seq3(gv), seq3(lab), reverse=True, o_fwd=o_fwd, gr=seq3(gr), gnorm=p["gnorm"])
    x2 = _outproj(x2, o_attn.reshape(b * s, ATTN_WIDTH), o_gla.reshape(b * s, GLA_VW), p["wo_a"], p["wo_g"],
                  p["n_post"])
    x2 = _ffn(x2, s, p["n_ffn_pre"], p["wup"], p["cw"], p["cb"], p["wdn"], p["n_ffn_post"])
    return x2.reshape(b, s, D_MODEL)


def kernel(x_prompt, x_sample, norm_mix_pre, w_in, w_gate_fwd, b_gate_fwd, w_gate_bwd, b_gate_bwd, gla_norm, w_out,
           norm_mix_post, norm_ffn_pre, w_up, conv_w, conv_b, w_down, norm_ffn_post):
    weights = (norm_mix_pre, w_in, w_gate_fwd, b_gate_fwd, w_gate_bwd, b_gate_bwd, gla_norm, w_out, norm_mix_post,
               norm_ffn_pre, w_up, conv_w, conv_b, w_down, norm_ffn_post)
    params = [_layer_params(l, *weights) for l in range(w_in.shape[0])]

    def trunk(x):
        for p in params:
            x = _layer(x, p)
        return x

    return (trunk(x_prompt), trunk(x_sample))
```

```python
import functools

import numpy as np
import jax
import jax.numpy as jnp
from jax import lax
from jax.experimental import pallas as pl
from jax.experimental.pallas import tpu as pltpu

F32 = jnp.float32
BF16 = jnp.bfloat16

D_MODEL = 1024
HEAD_DIM = 64
ATTN_HEADS = 8
ATTN_WIDTH = ATTN_HEADS * HEAD_DIM
GLA_HEADS = 8
GLA_DK = 32
GLA_DV = 64
GLA_KW = GLA_HEADS * GLA_DK
GLA_VW = GLA_HEADS * GLA_DV
GATE_RANK = 16
GATE_TAU = 16.0
GLA_CHUNK = 32
DIL_PAIRS = ((128, 1), (512, 4), (2048, 16))
DILATIONS = tuple(d for _, d in DIL_PAIRS)
N_SIDE = 64
D_FF = 2816
EPS = 1e-6
MASKED_DIST = 1e30

LANES = 128
HEAD_PAIRS = ATTN_HEADS // 2
VMEM_LIMIT = 56 * 1024 * 1024

TM_PROJ = 1024
ATT_BQ = 128
ATT_BK = ATT_BQ + 2 * N_SIDE
SEG_PAD = N_SIDE
ATT_UNROLL = 32
GLA_T = 256
GLA_TILES = 8
FF_CHUNK = 256


def _cparams(*sem):
    return pltpu.CompilerParams(dimension_semantics=sem, vmem_limit_bytes=VMEM_LIMIT)


def _rms(x, g):
    ms = jnp.mean(x * x, axis=-1, keepdims=True)
    return x * lax.rsqrt(ms + EPS) * g


def _const_spec(shape):
    nd = len(shape)
    return pl.BlockSpec(shape, lambda *_: (0,) * nd, pipeline_mode=pl.Buffered(1))


def _inproj_kernel(x_ref, g_ref, wm_ref, wlr_ref, wg_ref, bg_ref,
                   q_ref, k_ref, v_ref, gq_ref, gk_ref, gv_ref, gr_ref, laf_ref, lab_ref):
    h = _rms(x_ref[...], g_ref[...]).astype(BF16)

    def proj(lo, hi):
        return jnp.dot(h, wm_ref[:, lo:hi], preferred_element_type=F32)

    lr = jnp.dot(h, wlr_ref[...], preferred_element_type=F32)
    a = ATTN_WIDTH
    q_ref[...] = (proj(0, a) * (HEAD_DIM ** -0.5)).astype(BF16)
    k_ref[...] = proj(a, 2 * a).astype(BF16)
    v_ref[...] = proj(2 * a, 3 * a).astype(BF16)

    pre = bg_ref[...] + jnp.dot(lr.astype(BF16), wg_ref[...], preferred_element_type=F32)
    la = (jnp.minimum(pre, 0.0) - jnp.log(1.0 + jnp.exp(-jnp.abs(pre)))) * (1.0 / GATE_TAU)
    laf_ref[...] = la[:, :GLA_KW]
    lab_ref[...] = la[:, GLA_KW:]

    o = 3 * a
    gq_ref[...] = proj(o, o + GLA_KW)
    gk_ref[...] = proj(o + GLA_KW, o + 2 * GLA_KW)
    o += 2 * GLA_KW
    gv_ref[...] = proj(o, o + GLA_VW).astype(BF16)
    gr_ref[...] = proj(o + GLA_VW, o + 2 * GLA_VW)


def _inproj(x2, g, wm, wlr, wg, bg):
    n = x2.shape[0]
    tm = TM_PROJ
    row = lambda w: pl.BlockSpec((tm, w), lambda i: (i, 0))
    outs = [(ATTN_WIDTH, BF16)] * 3 + [(GLA_KW, F32)] * 2 + [(GLA_VW, BF16), (GLA_VW, F32)] + [(GLA_KW, F32)] * 2
    return pl.pallas_call(
        _inproj_kernel,
        grid=(n // tm,),
        in_specs=[row(D_MODEL), _const_spec(g.shape), _const_spec(wm.shape), _const_spec(wlr.shape),
                  _const_spec(wg.shape), _const_spec(bg.shape)],
        out_specs=[row(w) for w, _ in outs],
        out_shape=[jax.ShapeDtypeStruct((n, w), dt) for w, dt in outs],
        compiler_params=_cparams("parallel"),
        name="inproj",
    )(x2, g, wm, wlr, wg, bg)


def _attn_kernel(slope_ref, dist_ref, q_ref, k_ref, v_ref, o_ref,
                 tok_ref, r4_ref, b1_ref, b4_ref, b16_ref, acc1_ref, acc4_ref, bias_ref, *, seq):
    s_len = seq
    l4, l16 = s_len // 4, s_len // 16
    p4, p16 = l4 + 2 * SEG_PAD, l16 + 2 * SEG_PAD
    srcs = (q_ref, k_ref, v_ref)
    lane = lax.broadcasted_iota(jnp.int32, (1, LANES), 1)
    first_head = lane < HEAD_DIM
    slope = slope_ref[0]

    @pl.when(pl.program_id(1) == 0)
    def _():
        zpad = jnp.zeros((SEG_PAD, LANES), BF16)
        for buf_ref, n_seg, n in ((b1_ref, 1, s_len), (b4_ref, 4, l4), (b16_ref, 16, l16)):
            for a in range(3):
                for seg in range(n_seg):
                    base = seg * (n + 2 * SEG_PAD)
                    buf_ref[a, base:base + SEG_PAD, :] = zpad
                    buf_ref[a, base + SEG_PAD + n:base + 2 * SEG_PAD + n, :] = zpad
        for di, dil in enumerate(DILATIONS):
            for kind in range(dist_ref.shape[0]):
                for hh in range(2):
                    c = slope[:, hh * HEAD_DIM:hh * HEAD_DIM + 1] * float(dil)
                    bias_ref[di, kind, hh * ATT_BQ:(hh + 1) * ATT_BQ, :] = c * dist_ref[kind]

    def put_segment(buf_ref, a, base, rows, n):
        buf_ref[a, base + SEG_PAD:base + SEG_PAD + n, :] = rows

    for a in range(3):
        def upcast(i, carry, a=a):
            r0 = pl.multiple_of(i * 256, 256)
            tok_ref[a, pl.ds(r0, 256), :] = srcs[a][0, pl.ds(r0, 256), :].astype(F32)
            return carry

        lax.fori_loop(0, s_len // 256, upcast, 0)
        put_segment(b1_ref, a, 0, srcs[a][0], s_len)
        for r in range(4):
            t = tok_ref[a, pl.ds(r, l4, stride=4), :]
            r4_ref[a, r * l4:(r + 1) * l4, :] = t
            put_segment(b4_ref, a, r * p4, t.astype(BF16), l4)
        for r in range(4):
            for sub in range(4):
                t = r4_ref[a, pl.ds(r * l4 + sub, l16, stride=4), :]
                put_segment(b16_ref, a, (r + 4 * sub) * p16, t.astype(BF16), l16)

    def block(buf_ref, seg_base, m0, kind, di):
        krow = pl.multiple_of(seg_base + m0, N_SIDE)
        q = buf_ref[0, pl.ds(pl.multiple_of(krow + SEG_PAD, N_SIDE), ATT_BQ), :]
        k = buf_ref[1, pl.ds(krow, ATT_BK), :]
        v = buf_ref[2, pl.ds(krow, ATT_BK), :]
        zero = jnp.zeros_like(q)
        qs = jnp.concatenate([jnp.where(first_head, q, zero), jnp.where(first_head, zero, q)], axis=0)
        s = lax.dot_general(qs, k, (((1,), (1,)), ((), ())), preferred_element_type=F32) - bias_ref[di, kind]
        m = jnp.max(s, axis=-1, keepdims=True)
        p = jnp.exp(s - m).astype(BF16)
        pv = jnp.dot(p, jnp.concatenate([v, jnp.ones_like(v)], axis=1), preferred_element_type=F32)
        pick = lambda t: jnp.where(first_head, t[:ATT_BQ], t[ATT_BQ:])
        return pick(pv[:, :LANES]), pick(pv[:, LANES:]), pick(m)

    def kind_of(qb, nqb):
        return jnp.where(qb == 0, 1, 0) + jnp.where(qb == nqb - 1, 2, 0)

    def merge(a, b):
        m = jnp.maximum(a[2], b[2])
        ea, eb = jnp.exp(a[2] - m), jnp.exp(b[2] - m)
        return a[0] * ea + b[0] * eb, a[1] * ea + b[1] * eb, m

    def load(acc_ref, rows):
        return tuple(acc_ref[i, rows, :] for i in range(3))

    def store(acc_ref, rows, part):
        for i in range(3):
            acc_ref[i, rows, :] = part[i]

    n1, n4, n16 = s_len // ATT_BQ, l4 // ATT_BQ, l16 // ATT_BQ

    def branch1(qb, carry):
        m0 = pl.multiple_of(qb * ATT_BQ, ATT_BQ)
        store(acc1_ref, pl.ds(m0, ATT_BQ), block(b1_ref, 0, m0, kind_of(qb, n1), 0))
        return carry

    lax.fori_loop(0, n1, branch1, 0, unroll=ATT_UNROLL)

    def branch4(it, carry):
        r, qb = lax.div(it, n4), lax.rem(it, n4)
        m0 = pl.multiple_of(qb * ATT_BQ, ATT_BQ)
        row = pl.multiple_of(r * l4 + m0, ATT_BQ)
        store(acc4_ref, pl.ds(row, ATT_BQ), block(b4_ref, r * p4, m0, kind_of(qb, n4), 1))
        return carry

    lax.fori_loop(0, 4 * n4, branch4, 0, unroll=ATT_UNROLL)

    def branch16(it, carry):
        r, qb = lax.div(it, n16), lax.rem(it, n16)
        m0 = pl.multiple_of(qb * ATT_BQ, ATT_BQ)
        rows = pl.ds(lax.rem(r, 4) * l4 + lax.div(r, 4) + 4 * m0, ATT_BQ, stride=4)
        store(acc4_ref, rows, merge(load(acc4_ref, rows), block(b16_ref, r * p16, m0, kind_of(qb, n16), 2)))
        return carry

    lax.fori_loop(0, 16 * n16, branch16, 0, unroll=ATT_UNROLL)

    for r in range(4):
        def final(cb, carry, r=r):
            row = pl.multiple_of(r * l4 + cb * ATT_BQ, ATT_BQ)
            tok = pl.ds(r + 4 * ATT_BQ * cb, ATT_BQ, stride=4)
            num, den, _ = merge(load(acc1_ref, tok), load(acc4_ref, pl.ds(row, ATT_BQ)))
            o_ref[0, tok, :] = num / den
            return carry

        lax.fori_loop(0, n4, final, 0, unroll=True)


def _dist_table():
    i = np.arange(ATT_BQ)[:, None]
    j = np.arange(ATT_BK)[None, :]
    delta = j - N_SIDE - i
    ok = np.abs(delta) <= N_SIDE
    first = ok & ((delta >= 0) | (i >= N_SIDE))
    last = ok & ((delta < N_SIDE) | (i < N_SIDE))
    kinds = [ok, first, last, first & last]
    return jnp.asarray(np.stack([np.where(t, np.abs(delta), MASKED_DIST) for t in kinds]).astype(np.float32))


def _attention(q, k, v, slopes):
    b, s, _ = q.shape
    assert s % (16 * ATT_BQ) == 0
    blk = pl.BlockSpec((1, s, LANES), lambda p, bi: (bi, 0, p))
    dist = _dist_table()
    seg_rows = lambda d: d * (s // d + 2 * SEG_PAD)
    return pl.pallas_call(
        functools.partial(_attn_kernel, seq=s),
        grid=(HEAD_PAIRS, b),
        in_specs=[pl.BlockSpec((1, 1, LANES), lambda p, bi: (p, 0, 0)), _const_spec(dist.shape), blk, blk, blk],
        out_specs=blk,
        out_shape=jax.ShapeDtypeStruct((b, s, ATTN_WIDTH), F32),
        scratch_shapes=[pltpu.VMEM((3, s, LANES), F32), pltpu.VMEM((3, s, LANES), F32)]
        + [pltpu.VMEM((3, seg_rows(d), LANES), BF16) for d in (1, 4, 16)]
        + [pltpu.VMEM((3, s, LANES), F32)] * 2
        + [pltpu.VMEM((len(DILATIONS), dist.shape[0], 2 * ATT_BQ, ATT_BK), F32)],
        compiler_params=_cparams("arbitrary", "arbitrary"),
        name="attn",
    )(slopes, dist, q, k, v)


def _split2(x):
    hi = x.astype(BF16)
    return hi, (x - hi.astype(F32)).astype(BF16)


def _gla_kernel(*refs, reverse):
    state_ref = refs[-1]

    @pl.when(pl.program_id(1) == 0)
    def _():
        state_ref[...] = jnp.zeros_like(state_ref)

    def view(r, ti):
        return r.at[:, pl.ds(ti * GLA_T, GLA_T), :] if len(r.shape) == 3 else r

    order = range(GLA_TILES - 1, -1, -1) if reverse else range(GLA_TILES)
    tiles = [_gla_tile(*[view(r, ti) for r in refs[:-1]], state_ref, reverse=reverse) for ti in order]
    for _ in range(GLA_PHASES):
        for t in tiles:
            next(t, None)


GLA_PHASES = 3


def _gla_tile(*refs, reverse):
    if reverse:
        (triT_ref, q_ref, k_ref, v_ref, la_ref, of_ref, r_ref, gn_ref, o_ref, state_ref) = refs
    else:
        (triT_ref, q_ref, k_ref, v_ref, la_ref, o_ref, state_ref) = refs
    T, C, KW = GLA_T, GLA_CHUNK, GLA_KW
    nch, half, pairs = T // C, T // 2, GLA_HEADS // 2

    laT = la_ref[0].T
    bt = sum(jnp.dot(t, triT_ref[...], preferred_element_type=F32) for t in _split2(laT))
    bT, totT = bt[:, :T], bt[:, T:]
    kT = k_ref[0].T
    q_in = q_ref[0] * jnp.exp(bT.T) * (GLA_DK ** -0.5)
    q_bf = q_in.astype(BF16)
    k_intraT = kT * jnp.exp(-bT)
    k_stateT = (kT * jnp.exp(totT - bT)).astype(BF16)
    decayT = jnp.exp(totT)
    v = v_ref[0]
    yield

    lane_k = lax.broadcasted_iota(jnp.int32, (1, KW), 1)
    lane_v = lax.broadcasted_iota(jnp.int32, (1, LANES), 1)
    first_half = lane_v < GLA_DV
    row_head = lax.broadcasted_iota(jnp.int32, (KW, 1), 0) // GLA_DK
    si = lax.broadcasted_iota(jnp.int32, (half, T), 0)
    sj = lax.broadcasted_iota(jnp.int32, (half, T), 1) % half
    causal = (si // C == sj // C) & ((sj >= si) if reverse else (sj <= si))

    intra = [[None] * 2 for _ in range(pairs)]
    for t2 in range(2):
        kk = k_intraT[:, t2 * half:(t2 + 1) * half]
        qq = q_bf[t2 * half:(t2 + 1) * half, :]
        for p in range(pairs):
            rhs = jnp.concatenate([jnp.where(row_head == 2 * p, kk, 0.0), jnp.where(row_head == 2 * p + 1, kk, 0.0)],
                                  axis=1).astype(BF16)
            s = jnp.dot(qq, rhs, preferred_element_type=F32)
            att = jnp.where(causal, s, 0.0).astype(BF16)
            vp = v[t2 * half:(t2 + 1) * half, p * LANES:(p + 1) * LANES]
            vbd = jnp.concatenate([jnp.where(first_half, vp, jnp.zeros_like(vp)),
                                   jnp.where(first_half, jnp.zeros_like(vp), vp)], axis=0)
            intra[p][t2] = jnp.dot(att, vbd, preferred_element_type=F32)

    par_row = lax.broadcasted_iota(jnp.int32, (2 * GLA_DK, LANES), 0) // GLA_DK
    par_lane = lax.broadcasted_iota(jnp.int32, (2 * GLA_DK, LANES), 1) // GLA_DV
    parity = par_row == par_lane
    u = [[None] * pairs for _ in range(nch)]
    for p in range(pairs):
        vp = v[:, p * LANES:(p + 1) * LANES]
        zero = jnp.zeros((C, LANES), BF16)
        vexp = jnp.concatenate(
            [jnp.concatenate([vp[m * C:(m + 1) * C] if m == n else zero for m in range(nch)], axis=0) for n in range(nch)],
            axis=1)
        up = jnp.dot(k_stateT[p * 2 * GLA_DK:(p + 1) * 2 * GLA_DK, :], vexp, preferred_element_type=F32)
        for n in range(nch):
            u[n][p] = jnp.where(parity, up[:, n * LANES:(n + 1) * LANES], 0.0)
    yield

    state = state_ref[...]
    seen = [None] * nch
    order = range(nch - 1, -1, -1) if reverse else range(nch)
    for n in order:
        seen[n] = state.astype(BF16)
        dcol = jnp.broadcast_to(decayT[:, n * C:n * C + 1], (KW, LANES))
        state = dcol * state + jnp.concatenate(u[n], axis=0)
    state_ref[...] = state

    inter = [[None] * nch for _ in range(GLA_HEADS // 2)]
    for n in range(nch):
        qn = q_in[n * C:(n + 1) * C, :]
        lhs = jnp.concatenate([jnp.where(lane_k // (2 * GLA_DK) == p, qn, 0.0) for p in range(GLA_HEADS // 2)],
                              axis=0).astype(BF16)
        res = jnp.dot(lhs, seen[n], preferred_element_type=F32)
        for p in range(GLA_HEADS // 2):
            inter[p][n] = res[p * C:(p + 1) * C, :]
    o = jnp.concatenate([jnp.concatenate(intra[p], axis=0) + jnp.concatenate(inter[p], axis=0) for p in range(pairs)],
                        axis=1)

    if not reverse:
        o_ref[0] = o
        return
    o = o + of_ref[0]
    r = r_ref[0]
    outs = []
    for p in range(GLA_HEADS // 2):
        x = o[:, p * LANES:(p + 1) * LANES]
        sq = x * x
        first = lane_v < GLA_DV
        s0 = jnp.sum(jnp.where(first, sq, 0.0), axis=-1, keepdims=True)
        s1 = jnp.sum(jnp.where(first, 0.0, sq), axis=-1, keepdims=True)
        ms = jnp.where(first, s0, s1) * (1.0 / GLA_DV)
        y = x * lax.rsqrt(ms + EPS) * gn_ref[...]
        rp = r[:, p * LANES:(p + 1) * LANES]
        outs.append(y * rp / (1.0 + jnp.exp(-rp)))
    o_ref[0] = jnp.concatenate(outs, axis=1).astype(o_ref.dtype)


def _gla_consts(reverse):
    i = np.arange(GLA_T)
    same = (i[:, None] // GLA_CHUNK) == (i[None, :] // GLA_CHUNK)
    tri = same & ((i[None, :] >= i[:, None]) if reverse else (i[None, :] <= i[:, None]))
    triT = np.concatenate([tri.T, same], axis=1)
    return jnp.asarray(triT, BF16)


def _gla_dir(gq, gk, gv, la, reverse, o_fwd=None, gr=None, gnorm=None):
    b, s, _ = gq.shape
    step = GLA_TILES * GLA_T
    assert s % step == 0
    nt = s // step
    tile = (lambda bi, t: (bi, nt - 1 - t, 0)) if reverse else (lambda bi, t: (bi, t, 0))
    kw = pl.BlockSpec((1, step, GLA_KW), tile)
    vw = pl.BlockSpec((1, step, GLA_VW), tile)
    triT = _gla_consts(reverse)
    args = [triT, gq, gk, gv, la]
    specs = [_const_spec(triT.shape), kw, kw, vw, kw]
    if reverse:
        args += [o_fwd, gr, gnorm]
        specs += [vw, vw, _const_spec(gnorm.shape)]
    return pl.pallas_call(
        functools.partial(_gla_kernel, reverse=reverse),
        grid=(b, nt),
        in_specs=specs,
        out_specs=vw,
        out_shape=jax.ShapeDtypeStruct((b, s, GLA_VW), BF16 if reverse else F32),
        scratch_shapes=[pltpu.VMEM((GLA_KW, LANES), F32)],
        compiler_params=_cparams("parallel", "arbitrary"),
        name="gla_bwd" if reverse else "gla_fwd",
    )(*args)


def _outproj_kernel(x_ref, oa_ref, og_ref, wa_ref, wg_ref, g_ref, y_ref):
    mix = jnp.dot(oa_ref[...].astype(BF16), wa_ref[...], preferred_element_type=F32)
    mix = mix + jnp.dot(og_ref[...], wg_ref[...], preferred_element_type=F32)
    y_ref[...] = x_ref[...] + _rms(mix, g_ref[...])


def _outproj(x2, oa, og, wa, wg, g):
    n = x2.shape[0]
    tm = TM_PROJ
    row = lambda w: pl.BlockSpec((tm, w), lambda i: (i, 0))
    return pl.pallas_call(
        _outproj_kernel,
        grid=(n // tm,),
        in_specs=[row(D_MODEL), row(ATTN_WIDTH), row(GLA_VW), _const_spec(wa.shape), _const_spec(wg.shape),
                  _const_spec(g.shape)],
        out_specs=row(D_MODEL),
        out_shape=jax.ShapeDtypeStruct((n, D_MODEL), F32),
        compiler_params=_cparams("parallel"),
        name="outproj",
    )(x2, oa, og, wa, wg, g)


def _gelu_tanh(y):
    return 0.5 * y * (1.0 + jnp.tanh(0.7978845608028654 * (y + 0.044715 * (y * y * y))))


def _ffn_kernel(x_ref, xp_ref, xn_ref, g_ref, wup_ref, cw_ref, cb_ref, wdn_ref, gp_ref, y_ref, act_ref, *, tiles_per_seq):
    tm = x_ref.shape[0]
    i = pl.program_id(0)
    first = (i % tiles_per_seq) == 0
    last = (i % tiles_per_seq) == tiles_per_seq - 1
    x = x_ref[...]
    g = g_ref[...]
    h = _rms(x, g).astype(BF16)
    halo = _rms(jnp.concatenate([xp_ref[...], xn_ref[...]], axis=0), g).astype(BF16)
    row = lax.broadcasted_iota(jnp.int32, (tm, 1), 0)
    for lo in range(0, D_FF, FF_CHUNK):
        hi = min(lo + FF_CHUNK, D_FF)
        wa = wup_ref[:, lo:hi]
        a = jnp.dot(h, wa, preferred_element_type=F32)
        gate = jnp.dot(h, wup_ref[:, D_FF + lo:D_FF + hi], preferred_element_type=F32)
        ah = jnp.dot(halo, wa, preferred_element_type=F32)
        a_prev = jnp.where(first, 0.0, ah[7:8, :])
        a_next = jnp.where(last, 0.0, ah[8:9, :])
        a_dn = jnp.where(row == 0, a_prev, pltpu.roll(a, 1, axis=0))
        a_up = jnp.where(row == tm - 1, a_next, pltpu.roll(a, tm - 1, axis=0))
        cw = cw_ref[:, lo:hi]
        y = cb_ref[:, lo:hi] + a_dn * cw[0:1, :] + a * cw[1:2, :] + a_up * cw[2:3, :]
        act_ref[:, lo:hi] = (_gelu_tanh(y) * gate).astype(BF16)
    f = jnp.dot(act_ref[...], wdn_ref[...], preferred_element_type=F32)
    y_ref[...] = x + _rms(f, gp_ref[...])


def _ffn(x2, seq, g, wup, cw, cb, wdn, gp):
    n = x2.shape[0]
    tm = TM_PROJ
    assert seq % tm == 0
    per, nblk = tm // 8, n // 8
    consts = (g, wup, cw, cb, wdn, gp)
    return pl.pallas_call(
        functools.partial(_ffn_kernel, tiles_per_seq=seq // tm),
        grid=(n // tm,),
        in_specs=[pl.BlockSpec((tm, D_MODEL), lambda i: (i, 0)),
                  pl.BlockSpec((8, D_MODEL), lambda i: (jnp.maximum(i * per - 1, 0), 0)),
                  pl.BlockSpec((8, D_MODEL), lambda i: (jnp.minimum((i + 1) * per, nblk - 1), 0))]
        + [_const_spec(c.shape) for c in consts],
        out_specs=pl.BlockSpec((tm, D_MODEL), lambda i: (i, 0)),
        out_shape=jax.ShapeDtypeStruct((n, D_MODEL), F32),
        scratch_shapes=[pltpu.VMEM((tm, D_FF), BF16)],
        compiler_params=_cparams("parallel"),
        name="ffn",
    )(x2, x2, x2, *consts)


def _layer_params(l,norm_mix_pre, w_in, w_gate_fwd, b_gate_fwd, w_gate_bwd, b_gate_bwd, gla_norm, w_out,
                  norm_mix_post, norm_ffn_pre, w_up, conv_w, conv_b, w_down, norm_ffn_post):
    n_main = 3 * ATTN_WIDTH + 2 * GLA_KW + 2 * GLA_VW
    wl = w_in[l]
    row = lambda t: t.reshape(1, -1).astype(F32)
    wlr = jnp.pad(wl[:, n_main:], ((0, 0), (0, LANES - GATE_RANK))).astype(BF16)
    wg = jnp.pad(jnp.concatenate([w_gate_fwd[l], w_gate_bwd[l]], axis=1), ((0, LANES - GATE_RANK), (0, 0)))
    bg = jnp.concatenate([b_gate_fwd[l], b_gate_bwd[l]]).reshape(1, -1)
    slopes = jnp.exp2(-8.0 * jnp.arange(1, ATTN_HEADS + 1, dtype=F32) / ATTN_HEADS)
    return dict(
        n_pre=row(norm_mix_pre[l]), wm=wl[:, :n_main].astype(BF16), wlr=wlr, wg=wg.astype(BF16), bg=bg.astype(F32),
        slopes=jnp.repeat(slopes, HEAD_DIM).reshape(HEAD_PAIRS, 1, LANES),
        gnorm=jnp.tile(gla_norm[l].astype(F32), 2).reshape(1, LANES),
        wo_a=w_out[l][:ATTN_WIDTH].astype(BF16), wo_g=w_out[l][ATTN_WIDTH:].astype(BF16), n_post=row(norm_mix_post[l]),
        n_ffn_pre=row(norm_ffn_pre[l]), wup=w_up[l].astype(BF16), cw=conv_w[l].astype(F32), cb=row(conv_b[l]),
        wdn=w_down[l].astype(BF16), n_ffn_post=row(norm_ffn_post[l]),
    )


def _layer(x, p):
    b, s, _ = x.shape
    x2 = x.reshape(b * s, D_MODEL)
    q, k, v, gq, gk, gv, gr, laf, lab = _inproj(x2, p["n_pre"], p["wm"], p["wlr"], p["wg"], p["bg"])
    seq3 = lambda t: t.reshape(b, s, t.shape[-1])
    o_attn = _attention(seq3(q), seq3(k), seq3(v), p["slopes"])
    o_fwd = _gla_dir(seq3(gq), seq3(gk), seq3(gv), seq3(laf), reverse=False)
    o_gla = _gla_dir(seq3(gq), seq3(gk), seq3(gv), seq3(lab), reverse=True, o_fwd=o_fwd, gr=seq3(gr), gnorm=p["gnorm"])
    x2 = _outproj(x2, o_attn.reshape(b * s, ATTN_WIDTH), o_gla.reshape(b * s, GLA_VW), p["wo_a"], p["wo_g"],
                  p["n_post"])
    x2 = _ffn(x2, s, p["n_ffn_pre"], p["wup"], p["cw"], p["cb"], p["wdn"], p["n_ffn_post"])
    return x2.reshape(b, s, D_MODEL)


def kernel(x_prompt, x_sample, norm_mix_pre, w_in, w_gate_fwd, b_gate_fwd, w_gate_bwd, b_gate_bwd, gla_norm, w_out,
           norm_mix_post, norm_ffn_pre, w_up, conv_w, conv_b, w_down, norm_ffn_post):
    weights = (norm_mix_pre, w_in, w_gate_fwd, b_gate_fwd, w_gate_bwd, b_gate_bwd, gla_norm, w_out, norm_mix_post,
               norm_ffn_pre, w_up, conv_w, conv_b, w_down, norm_ffn_post)
    params = [_layer_params(l, *weights) for l in range(w_in.shape[0])]

    def trunk(x):
        for p in params:
            x = _layer(x, p)
        return x

    return (trunk(x_prompt), trunk(x_sample))
```

```python
import functools

import numpy as np
import jax
import jax.numpy as jnp
from jax import lax
from jax.experimental import pallas as pl
from jax.experimental.pallas import tpu as pltpu

F32 = jnp.float32
BF16 = jnp.bfloat16

D_MODEL = 1024
HEAD_DIM = 64
ATTN_HEADS = 8
ATTN_WIDTH = ATTN_HEADS * HEAD_DIM
GLA_HEADS = 8
GLA_DK = 32
GLA_DV = 64
GLA_KW = GLA_HEADS * GLA_DK
GLA_VW = GLA_HEADS * GLA_DV
GATE_RANK = 16
GATE_TAU = 16.0
GLA_CHUNK = 32
DIL_PAIRS = ((128, 1), (512, 4), (2048, 16))
DILATIONS = tuple(d for _, d in DIL_PAIRS)
N_SIDE = 64
D_FF = 2816
EPS = 1e-6
MASKED_DIST = 1e30

LANES = 128
HEAD_PAIRS = ATTN_HEADS // 2
VMEM_LIMIT = 56 * 1024 * 1024

TM_PROJ = 1024
ATT_BQ = 128
ATT_BK = ATT_BQ + 2 * N_SIDE
SEG_PAD = N_SIDE
ATT_UNROLL = 32
GLA_T = 256
GLA_TILES_FWD = 8
GLA_TILES_BWD = 4
FF_CHUNK = 256


def _cparams(*sem):
    return pltpu.CompilerParams(dimension_semantics=sem, vmem_limit_bytes=VMEM_LIMIT)


def _rms(x, g):
    ms = jnp.mean(x * x, axis=-1, keepdims=True)
    return x * lax.rsqrt(ms + EPS) * g


def _const_spec(shape):
    nd = len(shape)
    return pl.BlockSpec(shape, lambda *_: (0,) * nd, pipeline_mode=pl.Buffered(1))


def _inproj_kernel(x_ref, g_ref, wm_ref, wlr_ref, wg_ref, bg_ref,
                   q_ref, k_ref, v_ref, gq_ref, gk_ref, gv_ref, gr_ref, laf_ref, lab_ref):
    h = _rms(x_ref[...], g_ref[...]).astype(BF16)

    def proj(lo, hi):
        return jnp.dot(h, wm_ref[:, lo:hi], preferred_element_type=F32)

    lr = jnp.dot(h, wlr_ref[...], preferred_element_type=F32)
    a = ATTN_WIDTH
    q_ref[...] = (proj(0, a) * (HEAD_DIM ** -0.5)).astype(BF16)
    k_ref[...] = proj(a, 2 * a).astype(BF16)
    v_ref[...] = proj(2 * a, 3 * a).astype(BF16)

    pre = bg_ref[...] + jnp.dot(lr.astype(BF16), wg_ref[...], preferred_element_type=F32)
    la = (jnp.minimum(pre, 0.0) - jnp.log(1.0 + jnp.exp(-jnp.abs(pre)))) * (1.0 / GATE_TAU)
    laf_ref[...] = la[:, :GLA_KW]
    lab_ref[...] = la[:, GLA_KW:]

    o = 3 * a
    gq_ref[...] = proj(o, o + GLA_KW)
    gk_ref[...] = proj(o + GLA_KW, o + 2 * GLA_KW)
    o += 2 * GLA_KW
    gv_ref[...] = proj(o, o + GLA_VW).astype(BF16)
    gr_ref[...] = proj(o + GLA_VW, o + 2 * GLA_VW)


def _inproj(x2, g, wm, wlr, wg, bg):
    n = x2.shape[0]
    tm = TM_PROJ
    row = lambda w: pl.BlockSpec((tm, w), lambda i: (i, 0))
    outs = [(ATTN_WIDTH, BF16)] * 3 + [(GLA_KW, F32)] * 2 + [(GLA_VW, BF16), (GLA_VW, F32)] + [(GLA_KW, F32)] * 2
    return pl.pallas_call(
        _inproj_kernel,
        grid=(n // tm,),
        in_specs=[row(D_MODEL), _const_spec(g.shape), _const_spec(wm.shape), _const_spec(wlr.shape),
                  _const_spec(wg.shape), _const_spec(bg.shape)],
        out_specs=[row(w) for w, _ in outs],
        out_shape=[jax.ShapeDtypeStruct((n, w), dt) for w, dt in outs],
        compiler_params=_cparams("parallel"),
        name="inproj",
    )(x2, g, wm, wlr, wg, bg)


def _attn_kernel(slope_ref, dist_ref, q_ref, k_ref, v_ref, o_ref,
                 tok_ref, r4_ref, b1_ref, b4_ref, b16_ref, acc1_ref, acc4_ref, bias_ref, *, seq):
    s_len = seq
    l4, l16 = s_len // 4, s_len // 16
    p4, p16 = l4 + 2 * SEG_PAD, l16 + 2 * SEG_PAD
    srcs = (q_ref, k_ref, v_ref)
    lane = lax.broadcasted_iota(jnp.int32, (1, LANES), 1)
    first_head = lane < HEAD_DIM
    slope = slope_ref[0]

    @pl.when(pl.program_id(1) == 0)
    def _():
        zpad = jnp.zeros((SEG_PAD, LANES), BF16)
        for buf_ref, n_seg, n in ((b1_ref, 1, s_len), (b4_ref, 4, l4), (b16_ref, 16, l16)):
            for a in range(3):
                for seg in range(n_seg):
                    base = seg * (n + 2 * SEG_PAD)
                    buf_ref[a, base:base + SEG_PAD, :] = zpad
                    buf_ref[a, base + SEG_PAD + n:base + 2 * SEG_PAD + n, :] = zpad
        for di, dil in enumerate(DILATIONS):
            for kind in range(dist_ref.shape[0]):
                for hh in range(2):
                    c = slope[:, hh * HEAD_DIM:hh * HEAD_DIM + 1] * float(dil)
                    bias_ref[di, kind, hh * ATT_BQ:(hh + 1) * ATT_BQ, :] = c * dist_ref[kind]

    def put_segment(buf_ref, a, base, rows, n):
        buf_ref[a, base + SEG_PAD:base + SEG_PAD + n, :] = rows

    for a in range(3):
        def upcast(i, carry, a=a):
            r0 = pl.multiple_of(i * 256, 256)
            tok_ref[a, pl.ds(r0, 256), :] = srcs[a][0, pl.ds(r0, 256), :].astype(F32)
            return carry

        lax.fori_loop(0, s_len // 256, upcast, 0)
        put_segment(b1_ref, a, 0, srcs[a][0], s_len)
        for r in range(4):
            t = tok_ref[a, pl.ds(r, l4, stride=4), :]
            r4_ref[a, r * l4:(r + 1) * l4, :] = t
            put_segment(b4_ref, a, r * p4, t.astype(BF16), l4)
        for r in range(4):
            for sub in range(4):
                t = r4_ref[a, pl.ds(r * l4 + sub, l16, stride=4), :]
                put_segment(b16_ref, a, (r + 4 * sub) * p16, t.astype(BF16), l16)

    def block(buf_ref, seg_base, m0, kind, di):
        krow = pl.multiple_of(seg_base + m0, N_SIDE)
        q = buf_ref[0, pl.ds(pl.multiple_of(krow + SEG_PAD, N_SIDE), ATT_BQ), :]
        k = buf_ref[1, pl.ds(krow, ATT_BK), :]
        v = buf_ref[2, pl.ds(krow, ATT_BK), :]
        zero = jnp.zeros_like(q)
        qs = jnp.concatenate([jnp.where(first_head, q, zero), jnp.where(first_head, zero, q)], axis=0)
        s = lax.dot_general(qs, k, (((1,), (1,)), ((), ())), preferred_element_type=F32) - bias_ref[di, kind]
        m = jnp.max(s, axis=-1, keepdims=True)
        p = jnp.exp(s - m).astype(BF16)
        pv = jnp.dot(p, jnp.concatenate([v, jnp.ones_like(v)], axis=1), preferred_element_type=F32)
        pick = lambda t: jnp.where(first_head, t[:ATT_BQ], t[ATT_BQ:])
        return pick(pv[:, :LANES]), pick(pv[:, LANES:]), pick(m)

    def kind_of(qb, nqb):
        return jnp.where(qb == 0, 1, 0) + jnp.where(qb == nqb - 1, 2, 0)

    def merge(a, b):
        m = jnp.maximum(a[2], b[2])
        ea, eb = jnp.exp(a[2] - m), jnp.exp(b[2] - m)
        return a[0] * ea + b[0] * eb, a[1] * ea + b[1] * eb, m

    def load(acc_ref, rows):
        return tuple(acc_ref[i, rows, :] for i in range(3))

    def store(acc_ref, rows, part):
        for i in range(3):
            acc_ref[i, rows, :] = part[i]

    n1, n4, n16 = s_len // ATT_BQ, l4 // ATT_BQ, l16 // ATT_BQ

    def branch1(qb, carry):
        m0 = pl.multiple_of(qb * ATT_BQ, ATT_BQ)
        store(acc1_ref, pl.ds(m0, ATT_BQ), block(b1_ref, 0, m0, kind_of(qb, n1), 0))
        return carry

    lax.fori_loop(0, n1, branch1, 0, unroll=ATT_UNROLL)

    def branch4(it, carry):
        r, qb = lax.div(it, n4), lax.rem(it, n4)
        m0 = pl.multiple_of(qb * ATT_BQ, ATT_BQ)
        row = pl.multiple_of(r * l4 + m0, ATT_BQ)
        store(acc4_ref, pl.ds(row, ATT_BQ), block(b4_ref, r * p4, m0, kind_of(qb, n4), 1))
        return carry

    lax.fori_loop(0, 4 * n4, branch4, 0, unroll=ATT_UNROLL)

    def branch16(it, carry):
        r, qb = lax.div(it, n16), lax.rem(it, n16)
        m0 = pl.multiple_of(qb * ATT_BQ, ATT_BQ)
        rows = pl.ds(lax.rem(r, 4) * l4 + lax.div(r, 4) + 4 * m0, ATT_BQ, stride=4)
        store(acc4_ref, rows, merge(load(acc4_ref, rows), block(b16_ref, r * p16, m0, kind_of(qb, n16), 2)))
        return carry

    lax.fori_loop(0, 16 * n16, branch16, 0, unroll=ATT_UNROLL)

    for r in range(4):
        def final(cb, carry, r=r):
            row = pl.multiple_of(r * l4 + cb * ATT_BQ, ATT_BQ)
            tok = pl.ds(r + 4 * ATT_BQ * cb, ATT_BQ, stride=4)
            num, den, _ = merge(load(acc1_ref, tok), load(acc4_ref, pl.ds(row, ATT_BQ)))
            o_ref[0, tok, :] = num / den
            return carry

        lax.fori_loop(0, n4, final, 0, unroll=True)


def _dist_table():
    i = np.arange(ATT_BQ)[:, None]
    j = np.arange(ATT_BK)[None, :]
    delta = j - N_SIDE - i
    ok = np.abs(delta) <= N_SIDE
    first = ok & ((delta >= 0) | (i >= N_SIDE))
    last = ok & ((delta < N_SIDE) | (i < N_SIDE))
    kinds = [ok, first, last, first & last]
    return jnp.asarray(np.stack([np.where(t, np.abs(delta), MASKED_DIST) for t in kinds]).astype(np.float32))


def _attention(q, k, v, slopes):
    b, s, _ = q.shape
    assert s % (16 * ATT_BQ) == 0
    blk = pl.BlockSpec((1, s, LANES), lambda p, bi: (bi, 0, p))
    dist = _dist_table()
    seg_rows = lambda d: d * (s // d + 2 * SEG_PAD)
    return pl.pallas_call(
        functools.partial(_attn_kernel, seq=s),
        grid=(HEAD_PAIRS, b),
        in_specs=[pl.BlockSpec((1, 1, LANES), lambda p, bi: (p, 0, 0)), _const_spec(dist.shape), blk, blk, blk],
        out_specs=blk,
        out_shape=jax.ShapeDtypeStruct((b, s, ATTN_WIDTH), F32),
        scratch_shapes=[pltpu.VMEM((3, s, LANES), F32), pltpu.VMEM((3, s, LANES), F32)]
        + [pltpu.VMEM((3, seg_rows(d), LANES), BF16) for d in (1, 4, 16)]
        + [pltpu.VMEM((3, s, LANES), F32)] * 2
        + [pltpu.VMEM((len(DILATIONS), dist.shape[0], 2 * ATT_BQ, ATT_BK), F32)],
        compiler_params=_cparams("arbitrary", "arbitrary"),
        name="attn",
    )(slopes, dist, q, k, v)


def _split2(x):
    hi = x.astype(BF16)
    return hi, (x - hi.astype(F32)).astype(BF16)


def _gla_kernel(*refs, reverse):
    state_ref = refs[-1]
    n_tiles = refs[1].shape[1] // GLA_T

    @pl.when(pl.program_id(1) == 0)
    def _():
        state_ref[...] = jnp.zeros_like(state_ref)

    def view(r, ti):
        return r.at[:, pl.ds(ti * GLA_T, GLA_T), :] if len(r.shape) == 3 else r

    order = range(n_tiles - 1, -1, -1) if reverse else range(n_tiles)
    tiles = [_gla_tile(*[view(r, ti) for r in refs[:-1]], state_ref, reverse=reverse) for ti in order]
    for _ in range(GLA_PHASES):
        for t in tiles:
            next(t, None)


GLA_PHASES = 3


def _gla_tile(*refs, reverse):
    if reverse:
        (triT_ref, q_ref, k_ref, v_ref, la_ref, of_ref, r_ref, gn_ref, o_ref, state_ref) = refs
    else:
        (triT_ref, q_ref, k_ref, v_ref, la_ref, o_ref, state_ref) = refs
    T, C, KW = GLA_T, GLA_CHUNK, GLA_KW
    nch, half, pairs = T // C, T // 2, GLA_HEADS // 2

    laT = la_ref[0].T
    bt = sum(jnp.dot(t, triT_ref[...], preferred_element_type=F32) for t in _split2(laT))
    bT, totT = bt[:, :T], bt[:, T:]
    kT = k_ref[0].T
    q_in = q_ref[0] * jnp.exp(bT.T) * (GLA_DK ** -0.5)
    q_bf = q_in.astype(BF16)
    k_intraT = kT * jnp.exp(-bT)
    k_stateT = (kT * jnp.exp(totT - bT)).astype(BF16)
    decayT = jnp.exp(totT)
    v = v_ref[0].astype(BF16)
    yield

    lane_k = lax.broadcasted_iota(jnp.int32, (1, KW), 1)
    lane_v = lax.broadcasted_iota(jnp.int32, (1, LANES), 1)
    first_half = lane_v < GLA_DV
    row_head = lax.broadcasted_iota(jnp.int32, (KW, 1), 0) // GLA_DK
    si = lax.broadcasted_iota(jnp.int32, (half, T), 0)
    sj = lax.broadcasted_iota(jnp.int32, (half, T), 1) % half
    causal = (si // C == sj // C) & ((sj >= si) if reverse else (sj <= si))

    intra = [[None] * 2 for _ in range(pairs)]
    for t2 in range(2):
        kk = k_intraT[:, t2 * half:(t2 + 1) * half]
        qq = q_bf[t2 * half:(t2 + 1) * half, :]
        for p in range(pairs):
            rhs = jnp.concatenate([jnp.where(row_head == 2 * p, kk, 0.0), jnp.where(row_head == 2 * p + 1, kk, 0.0)],
                                  axis=1).astype(BF16)
            s = jnp.dot(qq, rhs, preferred_element_type=F32)
            att = jnp.where(causal, s, 0.0).astype(BF16)
            vp = v[t2 * half:(t2 + 1) * half, p * LANES:(p + 1) * LANES]
            vbd = jnp.concatenate([jnp.where(first_half, vp, jnp.zeros_like(vp)),
                                   jnp.where(first_half, jnp.zeros_like(vp), vp)], axis=0)
            intra[p][t2] = jnp.dot(att, vbd, preferred_element_type=F32)

    par_row = lax.broadcasted_iota(jnp.int32, (2 * GLA_DK, LANES), 0) // GLA_DK
    par_lane = lax.broadcasted_iota(jnp.int32, (2 * GLA_DK, LANES), 1) // GLA_DV
    parity = par_row == par_lane
    u = [[None] * pairs for _ in range(nch)]
    for p in range(pairs):
        vp = v[:, p * LANES:(p + 1) * LANES]
        zero = jnp.zeros((C, LANES), BF16)
        vexp = jnp.concatenate(
            [jnp.concatenate([vp[m * C:(m + 1) * C] if m == n else zero for m in range(nch)], axis=0) for n in range(nch)],
            axis=1)
        up = jnp.dot(k_stateT[p * 2 * GLA_DK:(p + 1) * 2 * GLA_DK, :], vexp, preferred_element_type=F32)
        for n in range(nch):
            u[n][p] = jnp.where(parity, up[:, n * LANES:(n + 1) * LANES], 0.0)
    yield

    state = state_ref[...]
    seen = [None] * nch
    order = range(nch - 1, -1, -1) if reverse else range(nch)
    for n in order:
        seen[n] = state.astype(BF16)
        dcol = jnp.broadcast_to(decayT[:, n * C:n * C + 1], (KW, LANES))
        state = dcol * state + jnp.concatenate(u[n], axis=0)
    state_ref[...] = state

    inter = [[None] * nch for _ in range(GLA_HEADS // 2)]
    for n in range(nch):
        qn = q_in[n * C:(n + 1) * C, :]
        lhs = jnp.concatenate([jnp.where(lane_k // (2 * GLA_DK) == p, qn, 0.0) for p in range(GLA_HEADS // 2)],
                              axis=0).astype(BF16)
        res = jnp.dot(lhs, seen[n], preferred_element_type=F32)
        for p in range(GLA_HEADS // 2):
            inter[p][n] = res[p * C:(p + 1) * C, :]
    o = jnp.concatenate([jnp.concatenate(intra[p], axis=0) + jnp.concatenate(inter[p], axis=0) for p in range(pairs)],
                        axis=1)

    if not reverse:
        o_ref[0] = o
        return
    o = o + of_ref[0]
    r = r_ref[0]
    outs = []
    for p in range(GLA_HEADS // 2):
        x = o[:, p * LANES:(p + 1) * LANES]
        sq = x * x
        first = lane_v < GLA_DV
        s0 = jnp.sum(jnp.where(first, sq, 0.0), axis=-1, keepdims=True)
        s1 = jnp.sum(jnp.where(first, 0.0, sq), axis=-1, keepdims=True)
        ms = jnp.where(first, s0, s1) * (1.0 / GLA_DV)
        y = x * lax.rsqrt(ms + EPS) * gn_ref[...]
        rp = r[:, p * LANES:(p + 1) * LANES]
        outs.append(y * rp / (1.0 + jnp.exp(-rp)))
    o_ref[0] = jnp.concatenate(outs, axis=1).astype(o_ref.dtype)


def _gla_consts(reverse):
    i = np.arange(GLA_T)
    same = (i[:, None] // GLA_CHUNK) == (i[None, :] // GLA_CHUNK)
    tri = same & ((i[None, :] >= i[:, None]) if reverse else (i[None, :] <= i[:, None]))
    triT = np.concatenate([tri.T, same], axis=1)
    return jnp.asarray(triT, BF16)


def _gla_dir(gq, gk, gv, la, reverse, o_fwd=None, gr=None, gnorm=None):
    b, s, _ = gq.shape
    step = (GLA_TILES_BWD if reverse else GLA_TILES_FWD) * GLA_T
    assert s % step == 0
    nt = s // step
    tile = (lambda bi, t: (bi, nt - 1 - t, 0)) if reverse else (lambda bi, t: (bi, t, 0))
    kw = pl.BlockSpec((1, step, GLA_KW), tile)
    vw = pl.BlockSpec((1, step, GLA_VW), tile)
    triT = _gla_consts(reverse)
    args = [triT, gq, gk, gv, la]
    specs = [_const_spec(triT.shape), kw, kw, vw, kw]
    if reverse:
        args += [o_fwd, gr, gnorm]
        specs += [vw, vw, _const_spec(gnorm.shape)]
    return pl.pallas_call(
        functools.partial(_gla_kernel, reverse=reverse),
        grid=(b, nt),
        in_specs=specs,
        out_specs=vw,
        out_shape=jax.ShapeDtypeStruct((b, s, GLA_VW), BF16 if reverse else F32),
        scratch_shapes=[pltpu.VMEM((GLA_KW, LANES), F32)],
        compiler_params=_cparams("parallel", "arbitrary"),
        name="gla_bwd" if reverse else "gla_fwd",
    )(*args)


def _outproj_kernel(x_ref, oa_ref, og_ref, wa_ref, wg_ref, g_ref, y_ref):
    mix = jnp.dot(oa_ref[...].astype(BF16), wa_ref[...], preferred_element_type=F32)
    mix = mix + jnp.dot(og_ref[...], wg_ref[...], preferred_element_type=F32)
    y_ref[...] = x_ref[...] + _rms(mix, g_ref[...])


def _outproj(x2, oa, og, wa, wg, g):
    n = x2.shape[0]
    tm = TM_PROJ
    row = lambda w: pl.BlockSpec((tm, w), lambda i: (i, 0))
    return pl.pallas_call(
        _outproj_kernel,
        grid=(n // tm,),
        in_specs=[row(D_MODEL), row(ATTN_WIDTH), row(GLA_VW), _const_spec(wa.shape), _const_spec(wg.shape),
                  _const_spec(g.shape)],
        out_specs=row(D_MODEL),
        out_shape=jax.ShapeDtypeStruct((n, D_MODEL), F32),
        compiler_params=_cparams("parallel"),
        name="outproj",
    )(x2, oa, og, wa, wg, g)


def _gelu_tanh(y):
    return 0.5 * y * (1.0 + jnp.tanh(0.7978845608028654 * (y + 0.044715 * (y * y * y))))


def _ffn_kernel(x_ref, xp_ref, xn_ref, g_ref, wup_ref, cw_ref, cb_ref, wdn_ref, gp_ref, y_ref, act_ref, *, tiles_per_seq):
    tm = x_ref.shape[0]
    i = pl.program_id(0)
    first = (i % tiles_per_seq) == 0
    last = (i % tiles_per_seq) == tiles_per_seq - 1
    x = x_ref[...]
    g = g_ref[...]
    h = _rms(x, g).astype(BF16)
    halo = _rms(jnp.concatenate([xp_ref[...], xn_ref[...]], axis=0), g).astype(BF16)
    row = lax.broadcasted_iota(jnp.int32, (tm, 1), 0)
    for lo in range(0, D_FF, FF_CHUNK):
        hi = min(lo + FF_CHUNK, D_FF)
        wa = wup_ref[:, lo:hi]
        a = jnp.dot(h, wa, preferred_element_type=F32)
        gate = jnp.dot(h, wup_ref[:, D_FF + lo:D_FF + hi], preferred_element_type=F32)
        ah = jnp.dot(halo, wa, preferred_element_type=F32)
        a_prev = jnp.where(first, 0.0, ah[7:8, :])
        a_next = jnp.where(last, 0.0, ah[8:9, :])
        a_dn = jnp.where(row == 0, a_prev, pltpu.roll(a, 1, axis=0))
        a_up = jnp.where(row == tm - 1, a_next, pltpu.roll(a, tm - 1, axis=0))
        cw = cw_ref[:, lo:hi]
        y = cb_ref[:, lo:hi] + a_dn * cw[0:1, :] + a * cw[1:2, :] + a_up * cw[2:3, :]
        act_ref[:, lo:hi] = (_gelu_tanh(y) * gate).astype(BF16)
    f = jnp.dot(act_ref[...], wdn_ref[...], preferred_element_type=F32)
    y_ref[...] = x + _rms(f, gp_ref[...])


def _ffn(x2, seq, g, wup, cw, cb, wdn, gp):
    n = x2.shape[0]
    tm = TM_PROJ
    assert seq % tm == 0
    per, nblk = tm // 8, n // 8
    consts = (g, wup, cw, cb, wdn, gp)
    return pl.pallas_call(
        functools.partial(_ffn_kernel, tiles_per_seq=seq // tm),
        grid=(n // tm,),
        in_specs=[pl.BlockSpec((tm, D_MODEL), lambda i: (i, 0)),
                  pl.BlockSpec((8, D_MODEL), lambda i: (jnp.maximum(i * per - 1, 0), 0)),
                  pl.BlockSpec((8, D_MODEL), lambda i: (jnp.minimum((i + 1) * per, nblk - 1), 0))]
        + [_const_spec(c.shape) for c in consts],
        out_specs=pl.BlockSpec((tm, D_MODEL), lambda i: (i, 0)),
        out_shape=jax.ShapeDtypeStruct((n, D_MODEL), F32),
        scratch_shapes=[pltpu.VMEM((tm, D_FF), BF16)],
        compiler_params=_cparams("parallel"),
        name="ffn",
    )(x2, x2, x2, *consts)


def _layer_params(l,norm_mix_pre, w_in, w_gate_fwd, b_gate_fwd, w_gate_bwd, b_gate_bwd, gla_norm, w_out,
                  norm_mix_post, norm_ffn_pre, w_up, conv_w, conv_b, w_down, norm_ffn_post):
    n_main = 3 * ATTN_WIDTH + 2 * GLA_KW + 2 * GLA_VW
    wl = w_in[l]
    row = lambda t: t.reshape(1, -1).astype(F32)
    wlr = jnp.pad(wl[:, n_main:], ((0, 0), (0, LANES - GATE_RANK))).astype(BF16)
    wg = jnp.pad(jnp.concatenate([w_gate_fwd[l], w_gate_bwd[l]], axis=1), ((0, LANES - GATE_RANK), (0, 0)))
    bg = jnp.concatenate([b_gate_fwd[l], b_gate_bwd[l]]).reshape(1, -1)
    slopes = jnp.exp2(-8.0 * jnp.arange(1, ATTN_HEADS + 1, dtype=F32) / ATTN_HEADS)
    return dict(
        n_pre=row(norm_mix_pre[l]), wm=wl[:, :n_main].astype(BF16), wlr=wlr, wg=wg.astype(BF16), bg=bg.astype(F32),
        slopes=jnp.repeat(slopes, HEAD_DIM).reshape(HEAD_PAIRS, 1, LANES),
        gnorm=jnp.tile(gla_norm[l].astype(F32), 2).reshape(1, LANES),
        wo_a=w_out[l][:ATTN_WIDTH].astype(BF16), wo_g=w_out[l][ATTN_WIDTH:].astype(BF16), n_post=row(norm_mix_post[l]),
        n_ffn_pre=row(norm_ffn_pre[l]), wup=w_up[l].astype(BF16), cw=conv_w[l].astype(F32), cb=row(conv_b[l]),
        wdn=w_down[l].astype(BF16), n_ffn_post=row(norm_ffn_post[l]),
    )


def _layer(x, p):
    b, s, _ = x.shape
    x2 = x.reshape(b * s, D_MODEL)
    q, k, v, gq, gk, gv, gr, laf, lab = _inproj(x2, p["n_pre"], p["wm"], p["wlr"], p["wg"], p["bg"])
    seq3 = lambda t: t.reshape(b, s, t.shape[-1])
    o_attn = _attention(seq3(q), seq3(k), seq3(v), p["slopes"])
    o_fwd = _gla_dir(seq3(gq), seq3(gk), seq3(gv), seq3(laf), reverse=False)
    o_gla = _gla_dir(seq3(gq), seq3(gk), seq3(gv), seq3(lab), reverse=True, o_fwd=o_fwd, gr=seq3(gr), gnorm=p["gnorm"])
    x2 = _outproj(x2, o_attn.reshape(b * s, ATTN_WIDTH), o_gla.reshape(b * s, GLA_VW), p["wo_a"], p["wo_g"],
                  p["n_post"])
    x2 = _ffn(x2, s, p["n_ffn_pre"], p["wup"], p["cw"], p["cb"], p["wdn"], p["n_ffn_post"])
    return x2.reshape(b, s, D_MODEL)


def kernel(x_prompt, x_sample, norm_mix_pre, w_in, w_gate_fwd, b_gate_fwd, w_gate_bwd, b_gate_bwd, gla_norm, w_out,
           norm_mix_post, norm_ffn_pre, w_up, conv_w, conv_b, w_down, norm_ffn_post):
    weights = (norm_mix_pre, w_in, w_gate_fwd, b_gate_fwd, w_gate_bwd, b_gate_bwd, gla_norm, w_out, norm_mix_post,
               norm_ffn_pre, w_up, conv_w, conv_b, w_down, norm_ffn_post)
    params = [_layer_params(l, *weights) for l in range(w_in.shape[0])]

    def trunk(x):
        for p in params:
            x = _layer(x, p)
        return x

    return (trunk(x_prompt), trunk(x_sample))
```
